```python
import jax
import jax.numpy as jnp
from jax import lax
import numpy as np

D_MODEL = 2048
BATCH = 4
SEQ = 2048
DEPTH = 2
DEC_BATCH = 128
DEC_SEQ = 8
PAST_LEN = 8192
PAGE_SIZE = 128

N_META = 16
Q_BLOCK = 128
RMS_EPS = 1e-6
LN_EPS = 1e-5
NEG_INF = -1e30

A_HEADS = 8
A_KV_HEADS = 4
A_GROUP = A_HEADS // A_KV_HEADS
A_HEAD_DIM = 128
A_WIDTH = A_HEADS * A_HEAD_DIM
A_KV_WIDTH = A_KV_HEADS * A_HEAD_DIM
A_SCALE = A_HEAD_DIM ** -0.5

POOL_WINDOWS = (2, 4, 8, 16)
POOL_MAX = 16
B_WIDTH = 1024
B_GROUP_WIDTH = B_WIDTH // len(POOL_WINDOWS)

C_HEADS = 8
C_NOPE_DIM = 128
C_ROPE_DIM = 64
C_V_DIM = 128
C_Q_LORA = 512
C_KV_LORA = 256
C_WIDTH = C_HEADS * C_V_DIM
C_SCALE = (C_NOPE_DIM + C_ROPE_DIM) ** -0.5
ROPE_THETA = 10000.0

D_CONV = 1024
CONV_W = 31

N_GROUPS = 4
EXPERTS_PER_GROUP = 8
N_EXPERTS = N_GROUPS * EXPERTS_PER_GROUP
TOP_K_IN_GROUP = 2
D_EXPERT = 512

IN_AB = A_WIDTH + 2 * A_KV_WIDTH + B_WIDTH
IN_CD = C_Q_LORA + C_KV_LORA + C_ROPE_DIM + 2 * D_CONV

kernel_name = 'hybrid_stickbreak_pool_mla_conformer_hmoe_step'


def _rmsnorm(x, g):
    xf = x.astype(jnp.float32)
    y = xf * lax.rsqrt(jnp.mean(xf * xf, axis=-1, keepdims=True) + RMS_EPS)
    return (y * g.astype(jnp.float32)).astype(x.dtype)


def _layernorm(x, g, b):
    xf = x.astype(jnp.float32)
    xc = xf - jnp.mean(xf, axis=-1, keepdims=True)
    y = xc * lax.rsqrt(jnp.mean(xc * xc, axis=-1, keepdims=True) + LN_EPS)
    return (y * g.astype(jnp.float32) + b.astype(jnp.float32)).astype(x.dtype)


def _query_blocks(t):
    bounds = [0] + list(range(N_META, t + 1, Q_BLOCK))
    if bounds[-1] != t:
        bounds.append(t)
    return list(zip(bounds[:-1], bounds[1:]))


def _sb_block(q, k, v, mask, acc):
    z = jnp.einsum('btkgd,bskd->bkgts', q.astype(jnp.float32), k.astype(jnp.float32)) * A_SCALE
    log_keep = jnp.where(mask, jax.nn.log_sigmoid(-z), 0.0)
    rev = lax.cumsum(log_keep, axis=4, reverse=True)
    later = jnp.concatenate([rev[..., 1:], jnp.zeros_like(rev[..., :1])], axis=-1) + acc[..., None]
    w = jnp.where(mask, jnp.exp(jax.nn.log_sigmoid(z) + later), 0.0)
    out = jnp.einsum('bkgts,bskd->btkgd', w, v.astype(jnp.float32))
    return out, acc + rev[..., 0]


def _pool_mix(u, prev, pos0, w_pool, pool_scale):
    b, t = u.shape[0], u.shape[1]
    u_ext = jnp.concatenate([prev.astype(u.dtype), u], axis=1)
    uf = u_ext.astype(jnp.float32)
    cs = jnp.concatenate([jnp.zeros_like(uf[:, :1]), lax.cumsum(uf, axis=1)], axis=1)
    pos = (pos0 + jnp.arange(t)).astype(jnp.float32)
    groups = []
    for gi, win in enumerate(POOL_WINDOWS):
        c0, c1 = gi * B_GROUP_WIDTH, (gi + 1) * B_GROUP_WIDTH
        wsum = cs[:, POOL_MAX:POOL_MAX + t, c0:c1] - cs[:, POOL_MAX - win:POOL_MAX - win + t, c0:c1]
        count = jnp.minimum(pos + 1.0, float(win))[None, :, None]
        groups.append(wsum / count - uf[:, POOL_MAX - 1:, c0:c1])
    d = jnp.stack(groups, axis=2)
    y = jnp.einsum('btgc,gce->btge', d, w_pool.astype(jnp.float32)).reshape(b, t, B_WIDTH)
    y = y * pool_scale.astype(jnp.float32)
    return y.astype(u.dtype), u_ext[:, -(POOL_MAX - 1):]


def _rope(x, pos):
    half = C_ROPE_DIM // 2
    inv_freq = ROPE_THETA ** (-jnp.arange(half, dtype=jnp.float32) / half)
    ang = pos.astype(jnp.float32)[:, None] * inv_freq[None, :]
    cos = jnp.cos(ang)[None, :, None, :]
    sin = jnp.sin(ang)[None, :, None, :]
    xf = x.astype(jnp.float32)
    x1, x2 = xf[..., :half], xf[..., half:]
    return jnp.concatenate([x1 * cos - x2 * sin, x2 * cos + x1 * sin], axis=-1).astype(x.dtype)


def _mla_project(cq_raw, ckv_raw, kpe_raw, pos, q_norm, kv_norm, w_uq, w_uk):
    b, t = cq_raw.shape[0], cq_raw.shape[1]
    q = (_rmsnorm(cq_raw, q_norm) @ w_uq).reshape(b, t, C_HEADS, C_NOPE_DIM + C_ROPE_DIM)
    q_pe = _rope(q[..., C_NOPE_DIM:], pos)
    q_lat = jnp.einsum('bthn,chn->bthc', q[..., :C_NOPE_DIM], w_uk)
    ckv = _rmsnorm(ckv_raw, kv_norm)
    kpe = _rope(kpe_raw[:, :, None, :], pos)[:, :, 0]
    return q_lat, q_pe, ckv, kpe


def _mla_attend(q_lat, q_pe, ckv, kpe, mask, w_uv):
    b, t = q_lat.shape[0], q_lat.shape[1]
    f = jnp.float32
    s = (jnp.einsum('bthc,bsc->bhts', q_lat.astype(f), ckv.astype(f))
         + jnp.einsum('bthr,bsr->bhts', q_pe.astype(f), kpe.astype(f))) * C_SCALE
    p = jax.nn.softmax(jnp.where(mask, s, NEG_INF), axis=-1)
    o = jnp.einsum('bhts,bsc->bthc', p, ckv.astype(f))
    o = jnp.einsum('bthc,chv->bthv', o, w_uv.astype(f))
    return o.reshape(b, t, C_WIDTH)


def _conv_module(glu_in, prev, conv_w, conv_b, norm_g, norm_b):
    u = glu_in[..., :D_CONV] * jax.nn.sigmoid(glu_in[..., D_CONV:])
    u_ext = jnp.concatenate([prev.astype(u.dtype), u], axis=1)
    y = lax.conv_general_dilated(u_ext, conv_w[:, None, :].astype(u.dtype), window_strides=(1,),
                                 padding='VALID', dimension_numbers=('NWC', 'WIO', 'NWC'),
                                 feature_group_count=D_CONV) + conv_b.astype(u.dtype)
    y = jax.nn.silu(_layernorm(y, norm_g, norm_b))
    return y, u_ext[:, -(CONV_W - 1):]


def _hier_moe(h, wg, bg, we, be, w_gate, w_up, w_down):
    b, t, d = h.shape
    n = h.reshape(b * t, d)
    g_logit = (n @ wg).astype(jnp.float32) + bg.astype(jnp.float32)
    g_prob = jax.nn.softmax(g_logit, axis=-1)
    grp = jnp.argmax(g_logit, axis=-1)
    g_w = jnp.take_along_axis(g_prob, grp[:, None], axis=1)
    e_logit = ((n @ we).astype(jnp.float32) + be.astype(jnp.float32)).reshape(-1, N_GROUPS, EXPERTS_PER_GROUP)
    e_in = jnp.take_along_axis(e_logit, grp[:, None, None], axis=1)[:, 0]
    top_v, top_i = lax.top_k(e_in, TOP_K_IN_GROUP)
    gates = g_w * jax.nn.softmax(top_v, axis=-1)
    expert = grp[:, None] * EXPERTS_PER_GROUP + top_i
    combine = jnp.sum(jax.nn.one_hot(expert, N_EXPERTS, dtype=jnp.float32) * gates[..., None], axis=1)
    hid = jax.nn.silu(jnp.einsum('nd,edf->nef', n, w_gate)) * jnp.einsum('nd,edf->nef', n, w_up)
    out = jnp.einsum('nef,efd->nd', hid * combine[..., None].astype(hid.dtype), w_down)
    return out.reshape(b, t, d).astype(h.dtype)


def _mixer_ab(hp, hs, cache_sb_k, cache_sb_v, state_pool, page_table, w_in_ab, w_pool, pool_scale, w_out_ab):
    def project(h):
        b, t = h.shape[0], h.shape[1]
        z = h @ w_in_ab
        q = z[..., :A_WIDTH].reshape(b, t, A_KV_HEADS, A_GROUP, A_HEAD_DIM)
        k = z[..., A_WIDTH:A_WIDTH + A_KV_WIDTH].reshape(b, t, A_KV_HEADS, A_HEAD_DIM)
        v = z[..., A_WIDTH + A_KV_WIDTH:A_WIDTH + 2 * A_KV_WIDTH].reshape(b, t, A_KV_HEADS, A_HEAD_DIM)
        return q, k, v, z[..., A_WIDTH + 2 * A_KV_WIDTH:]

    b, t = hp.shape[0], hp.shape[1]
    qp, kp, vp, up = project(hp)
    outs = []
    for t0, t1 in _query_blocks(t):
        mask = jnp.arange(t1)[None, :] < jnp.arange(t0, t1)[:, None]
        o, _ = _sb_block(qp[:, t0:t1], kp[:, :t1], vp[:, :t1], mask,
                         jnp.zeros((b, A_KV_HEADS, A_GROUP, t1 - t0), jnp.float32))
        outs.append(o)
    oa_p = jnp.concatenate(outs, axis=1).reshape(b, t, A_WIDTH).astype(hp.dtype)
    ob_p, pool_p = _pool_mix(up, jnp.zeros((b, POOL_MAX - 1, B_WIDTH), up.dtype), 0, w_pool, pool_scale)
    mix_p = jnp.concatenate([oa_p, ob_p], axis=-1) @ w_out_ab

    bs, ts = hs.shape[0], hs.shape[1]
    qs, ks, vs, us = project(hs)
    mask_new = jnp.arange(ts)[None, :] < jnp.arange(ts)[:, None]
    o, acc = _sb_block(qs, ks, vs, mask_new, jnp.zeros((bs, A_KV_HEADS, A_GROUP, ts), jnp.float32))
    page_mask = jnp.ones((ts, PAGE_SIZE), jnp.bool_)

    def page_step(carry, phys):
        o_c, acc_c = carry
        do, acc_c = _sb_block(qs, cache_sb_k[phys], cache_sb_v[phys], page_mask, acc_c)
        return (o_c + do, acc_c), None

    (o, _), _ = lax.scan(page_step, (o, acc), page_table.T[::-1])
    oa_s = o.reshape(bs, ts, A_WIDTH).astype(hs.dtype)
    ob_s, pool_s = _pool_mix(us, state_pool, PAST_LEN, w_pool, pool_scale)
    mix_s = jnp.concatenate([oa_s, ob_s], axis=-1) @ w_out_ab
    return mix_p, mix_s, kp, vp, ks, vs, pool_p, pool_s


def _mixer_cd(hp, hs, cache_mla_ckv, cache_mla_kpe, state_conv, page_table, w_in_cd, mla_q_norm, mla_kv_norm,
              w_uq, w_uk, w_uv, conv_w, conv_b, conv_norm_g, conv_norm_b, w_out_cd):
    o1 = C_Q_LORA
    o2 = o1 + C_KV_LORA
    o3 = o2 + C_ROPE_DIM

    def project(h, pos):
        z = h @ w_in_cd
        q_lat, q_pe, ckv, kpe = _mla_project(z[..., :o1], z[..., o1:o2], z[..., o2:o3], pos,
                                             mla_q_norm, mla_kv_norm, w_uq, w_uk)
        return q_lat, q_pe, ckv, kpe, z[..., o3:]

    b, t = hp.shape[0], hp.shape[1]
    q_lat, q_pe, ckv_p, kpe_p, glu_p = project(hp, jnp.arange(t))
    outs = []
    for t0, t1 in _query_blocks(t):
        mask = jnp.arange(t1)[None, :] <= jnp.arange(t0, t1)[:, None]
        outs.append(_mla_attend(q_lat[:, t0:t1], q_pe[:, t0:t1], ckv_p[:, :t1], kpe_p[:, :t1], mask, w_uv))
    oc_p = jnp.concatenate(outs, axis=1).astype(hp.dtype)
    od_p, conv_p = _conv_module(glu_p, jnp.zeros((b, CONV_W - 1, D_CONV), glu_p.dtype),
                                conv_w, conv_b, conv_norm_g, conv_norm_b)
    mix_p = jnp.concatenate([oc_p, od_p], axis=-1) @ w_out_cd

    bs, ts = hs.shape[0], hs.shape[1]
    q_lat_s, q_pe_s, ckv_s, kpe_s, glu_s = project(hs, PAST_LEN + jnp.arange(ts))
    ckv_past = jnp.take(cache_mla_ckv, page_table, axis=0).reshape(bs, -1, C_KV_LORA)
    kpe_past = jnp.take(cache_mla_kpe, page_table, axis=0).reshape(bs, -1, C_ROPE_DIM)
    ckv_all = jnp.concatenate([ckv_past.astype(ckv_s.dtype), ckv_s], axis=1)
    kpe_all = jnp.concatenate([kpe_past.astype(kpe_s.dtype), kpe_s], axis=1)
    past = ckv_past.shape[1]
    mask = jnp.concatenate([jnp.ones((ts, past), jnp.bool_), jnp.tril(jnp.ones((ts, ts), jnp.bool_))], axis=1)
    oc_s = _mla_attend(q_lat_s, q_pe_s, ckv_all, kpe_all, mask, w_uv).astype(hs.dtype)
    od_s, conv_s = _conv_module(glu_s, state_conv, conv_w, conv_b, conv_norm_g, conv_norm_b)
    mix_s = jnp.concatenate([oc_s, od_s], axis=-1) @ w_out_cd
    return mix_p, mix_s, ckv_p, kpe_p, ckv_s, kpe_s, conv_p, conv_s


def setup_inputs(seed: int = 0) -> dict:
    key = jax.random.key(seed)
    keys = iter(jax.random.split(key, 48))

    def nrm(shape, scale=1.0):
        return jax.random.normal(next(keys), shape, jnp.float32) * scale

    def gain(shape):
        return 1.0 + nrm(shape, 0.02)

    n_pages = PAST_LEN // PAGE_SIZE
    n_pool = (DEC_BATCH * n_pages * 5) // 4
    perm = jax.random.permutation(next(keys), n_pool)
    page_table = perm[:DEC_BATCH * n_pages].reshape(DEC_BATCH, n_pages).astype(jnp.int32)
    return {
        'x_prompt': nrm((BATCH, SEQ, D_MODEL)),
        'x_sample': nrm((DEC_BATCH, DEC_SEQ, D_MODEL)),
        'cache_sb_k': nrm((n_pool, PAGE_SIZE, A_KV_HEADS, A_HEAD_DIM)),
        'cache_sb_v': nrm((n_pool, PAGE_SIZE, A_KV_HEADS, A_HEAD_DIM)),
        'state_pool': nrm((DEC_BATCH, POOL_MAX - 1, B_WIDTH)),
        'cache_mla_ckv': nrm((n_pool, PAGE_SIZE, C_KV_LORA)),
        'cache_mla_kpe': nrm((n_pool, PAGE_SIZE, C_ROPE_DIM)),
        'state_conv': nrm((DEC_BATCH, CONV_W - 1, D_CONV), 0.5),
        'page_table': page_table,
        'meta_tokens': nrm((N_META, D_MODEL)),
        'norm_mix': gain((DEPTH, D_MODEL)),
        'norm_ffn': gain((DEPTH, D_MODEL)),
        'norm_final': gain((D_MODEL,)),
        'w_in_ab': nrm((D_MODEL, IN_AB), D_MODEL ** -0.5),
        'w_pool': nrm((len(POOL_WINDOWS), B_GROUP_WIDTH, B_GROUP_WIDTH), B_GROUP_WIDTH ** -0.5),
        'pool_scale': gain((B_WIDTH,)),
        'w_out_ab': nrm((A_WIDTH + B_WIDTH, D_MODEL), (A_WIDTH + B_WIDTH) ** -0.5),
        'w_in_cd': nrm((D_MODEL, IN_CD), D_MODEL ** -0.5),
        'mla_q_norm': gain((C_Q_LORA,)),
        'mla_kv_norm': gain((C_KV_LORA,)),
        'w_uq': nrm((C_Q_LORA, C_HEADS * (C_NOPE_DIM + C_ROPE_DIM)), C_Q_LORA ** -0.5),
        'w_uk': nrm((C_KV_LORA, C_HEADS, C_NOPE_DIM), C_KV_LORA ** -0.5),
        'w_uv': nrm((C_KV_LORA, C_HEADS, C_V_DIM), C_KV_LORA ** -0.5),
        'conv_w': nrm((CONV_W, D_CONV), CONV_W ** -0.5),
        'conv_b': nrm((D_CONV,), 0.02),
        'conv_norm_g': gain((D_CONV,)),
        'conv_norm_b': nrm((D_CONV,), 0.02),
        'w_out_cd': nrm((C_WIDTH + D_CONV, D_MODEL), (C_WIDTH + D_CONV) ** -0.5),
        'router_group': nrm((DEPTH, D_MODEL, N_GROUPS), D_MODEL ** -0.5),
        'router_group_bias': nrm((DEPTH, N_GROUPS), 0.01),
        'router_expert': nrm((DEPTH, D_MODEL, N_EXPERTS), D_MODEL ** -0.5),
        'router_expert_bias': nrm((DEPTH, N_EXPERTS), 0.01),
        'expert_w_gate': nrm((DEPTH, N_EXPERTS, D_MODEL, D_EXPERT), D_MODEL ** -0.5),
        'expert_w_up': nrm((DEPTH, N_EXPERTS, D_MODEL, D_EXPERT), D_MODEL ** -0.5),
        'expert_w_down': nrm((DEPTH, N_EXPERTS, D_EXPERT, D_MODEL), D_EXPERT ** -0.5),
    }


def reference(x_prompt, x_sample, cache_sb_k, cache_sb_v, state_pool, cache_mla_ckv, cache_mla_kpe, state_conv,
              page_table, meta_tokens, norm_mix, norm_ffn, norm_final, w_in_ab, w_pool, pool_scale, w_out_ab,
              w_in_cd, mla_q_norm, mla_kv_norm, w_uq, w_uk, w_uv, conv_w, conv_b, conv_norm_g, conv_norm_b,
              w_out_cd, router_group, router_group_bias, router_expert, router_expert_bias,
              expert_w_gate, expert_w_up, expert_w_down):
    b = x_prompt.shape[0]
    meta = jnp.broadcast_to(meta_tokens.astype(x_prompt.dtype)[None], (b, N_META, D_MODEL))
    xp = jnp.concatenate([meta, x_prompt], axis=1)
    xs = x_sample
    for layer in range(DEPTH):
        hp = _rmsnorm(xp, norm_mix[layer])
        hs = _rmsnorm(xs, norm_mix[layer])
        if layer % 2 == 0:
            mp, ms, sb_k_p, sb_v_p, sb_k_s, sb_v_s, pool_p, pool_s = _mixer_ab(
                hp, hs, cache_sb_k, cache_sb_v, state_pool, page_table, w_in_ab, w_pool, pool_scale, w_out_ab)
        else:
            mp, ms, ckv_p, kpe_p, ckv_s, kpe_s, conv_p, conv_s = _mixer_cd(
                hp, hs, cache_mla_ckv, cache_mla_kpe, state_conv, page_table, w_in_cd, mla_q_norm, mla_kv_norm,
                w_uq, w_uk, w_uv, conv_w, conv_b, conv_norm_g, conv_norm_b, w_out_cd)
        xp = xp + mp
        xs = xs + ms
        moe = (router_group[layer], router_group_bias[layer], router_expert[layer], router_expert_bias[layer],
               expert_w_gate[layer], expert_w_up[layer], expert_w_down[layer])
        xp = xp + _hier_moe(_rmsnorm(xp, norm_ffn[layer]), *moe)
        xs = xs + _hier_moe(_rmsnorm(xs, norm_ffn[layer]), *moe)
    y_prompt = _rmsnorm(xp, norm_final)[:, N_META:]
    y_sample = _rmsnorm(xs, norm_final)
    return (y_prompt, y_sample, sb_k_p, sb_v_p, sb_k_s, sb_v_s, pool_p, pool_s,
            ckv_p, kpe_p, ckv_s, kpe_s, conv_p, conv_s)
```

```python
import functools

import jax
import jax.numpy as jnp
from jax import lax
from jax.experimental import pallas as pl
from jax.experimental.pallas import tpu as pltpu

F32 = jnp.float32
BF16 = jnp.bfloat16
I32 = jnp.int32

D_MODEL = 2048
N_META = 16
RMS_EPS = 1e-6
LN_EPS = 1e-5
NEG_INF = -1e30
PAGE = 128
LANES = 128
QB = 128

A_HEADS = 8
A_KV_HEADS = 4
A_HEAD_DIM = 128
A_WIDTH = A_HEADS * A_HEAD_DIM
A_KV_WIDTH = A_KV_HEADS * A_HEAD_DIM
A_SCALE = A_HEAD_DIM ** -0.5

POOL_WINDOWS = (2, 4, 8, 16)
POOL_MAX = 16
B_WIDTH = 1024
B_GROUP_WIDTH = B_WIDTH // len(POOL_WINDOWS)

C_HEADS = 8
C_NOPE_DIM = 128
C_ROPE_DIM = 64
C_V_DIM = 128
C_Q_LORA = 512
C_KV_LORA = 256
C_WIDTH = C_HEADS * C_V_DIM
C_SCALE = (C_NOPE_DIM + C_ROPE_DIM) ** -0.5
ROPE_THETA = 10000.0
C_QK = C_KV_LORA + LANES

D_CONV = 1024
CONV_W = 31
CONV_HALO = 32

N_GROUPS = 4
EXPERTS_PER_GROUP = 8
N_EXPERTS = 32
D_EXPERT = 512
MOE_TILE = 256


def _cparams(sem, vmem_mb):
    return pltpu.CompilerParams(dimension_semantics=sem, vmem_limit_bytes=vmem_mb << 20)


def _rms(x, g):
    return x * lax.rsqrt(jnp.mean(x * x, axis=-1, keepdims=True) + RMS_EPS) * g


def _dot(a, b):
    return jnp.dot(a, b, preferred_element_type=F32)


def _dot_nt(a, b):
    return lax.dot_general(a, b, (((1,), (1,)), ((), ())), preferred_element_type=F32)


def _softplus(z):
    return jnp.maximum(z, 0.0) + jnp.log(1.0 + jnp.exp(-jnp.abs(z)))


def _moe_combine(x_ref, o1_ref, o2_ref, rf_ref):
    rf = rf_ref[...]
    return x_ref[...] + rf[:, 0:1] * o1_ref[...] + rf[:, 1:2] * o2_ref[...]


def _in_ab_kernel(x_ref, g_ref, w_ref, q_ref, k_ref, v_ref, u_ref):
    h = _rms(x_ref[...], g_ref[...]).astype(BF16)
    q_ref[...] = _dot(h, w_ref[:, 0:A_WIDTH]) * A_SCALE
    k_ref[...] = _dot(h, w_ref[:, A_WIDTH:A_WIDTH + A_KV_WIDTH])
    v_ref[...] = _dot(h, w_ref[:, A_WIDTH + A_KV_WIDTH:A_WIDTH + 2 * A_KV_WIDTH])
    u_ref[...] = _dot(h, w_ref[:, A_WIDTH + 2 * A_KV_WIDTH:])


def _in_proj_ab(x, g, w, tm):
    nt = x.shape[0]
    n_in = w.shape[1]
    row = lambda i: (i, 0)
    const = lambda i: (0, 0)
    return pl.pallas_call(
        _in_ab_kernel,
        grid=(nt // tm,),
        in_specs=[pl.BlockSpec((tm, D_MODEL), row), pl.BlockSpec((1, D_MODEL), const),
                  pl.BlockSpec((D_MODEL, n_in), const)],
        out_specs=[pl.BlockSpec((tm, A_WIDTH), row), pl.BlockSpec((tm, A_KV_WIDTH), row),
                   pl.BlockSpec((tm, A_KV_WIDTH), row), pl.BlockSpec((tm, B_WIDTH), row)],
        out_shape=[jax.ShapeDtypeStruct((nt, A_WIDTH), F32), jax.ShapeDtypeStruct((nt, A_KV_WIDTH), F32),
                   jax.ShapeDtypeStruct((nt, A_KV_WIDTH), F32), jax.ShapeDtypeStruct((nt, B_WIDTH), F32)],
        compiler_params=_cparams(("parallel",), 48),
        name="in_proj_ab",
    )(x, g, w)


def _rot_half(x, lane_period):
    n = x.shape[-1]
    half = C_ROPE_DIM // 2
    lane = lax.broadcasted_iota(I32, x.shape, x.ndim - 1) % lane_period
    fwd = pltpu.roll(x, n - half, x.ndim - 1)
    bwd = pltpu.roll(x, half, x.ndim - 1)
    return jnp.where(lane < half, -fwd, jnp.where(lane < C_ROPE_DIM, bwd, 0.0))


def _in_cd_kernel(x_ref, o1_ref, o2_ref, rf_ref, g_ref, w_ref, qn_ref, kvn_ref, wuq_ref, wuk_ref, cos_ref, sin_ref,
                  xo_ref, qc_ref, ckv_ref, kpe_ref, kc_ref, u_ref):
    x = _moe_combine(x_ref, o1_ref, o2_ref, rf_ref)
    xo_ref[...] = x
    h = _rms(x, g_ref[...]).astype(BF16)
    cos = cos_ref[...]
    sin = sin_ref[...]
    cq = _rms(_dot(h, w_ref[:, 0:C_Q_LORA]), qn_ref[...]).astype(BF16)
    for hd in range(C_HEADS):
        qn = _dot(cq, wuq_ref[:, hd * LANES:(hd + 1) * LANES]).astype(BF16)
        qc_ref[:, hd * C_QK:hd * C_QK + C_KV_LORA] = _dot(qn, wuk_ref[hd]) * C_SCALE
        qp = _dot(cq, wuq_ref[:, (C_HEADS + hd) * LANES:(C_HEADS + hd + 1) * LANES])
        qc_ref[:, hd * C_QK + C_KV_LORA:(hd + 1) * C_QK] = (qp * cos + _rot_half(qp, LANES) * sin) * C_SCALE
    o1 = C_Q_LORA
    o2 = o1 + C_KV_LORA
    ckv = _rms(_dot(h, w_ref[:, o1:o2]), kvn_ref[...])
    ckv_ref[...] = ckv
    o3 = o2 + 2 * D_CONV
    kr = _dot(h, w_ref[:, o3:o3 + LANES])
    kpe = kr * cos + _rot_half(kr, LANES) * sin
    kpe_ref[...] = kpe[:, 0:C_ROPE_DIM]
    kc_ref[:, 0:C_KV_LORA] = ckv.astype(BF16)
    kc_ref[:, C_KV_LORA:] = kpe.astype(BF16)
    ga = _dot(h, w_ref[:, o2:o2 + D_CONV])
    gb = _dot(h, w_ref[:, o2 + D_CONV:o3])
    u_ref[...] = ga * jax.nn.sigmoid(gb)


def _in_proj_cd(x, o1, o2, rf, g, w, qn, kvn, wuq, wuk, cos, sin, tm):
    nt = x.shape[0]
    nb = nt // tm
    row = lambda i: (i, 0)
    row2 = lambda i: (i + nb, 0)
    const = lambda i: (0, 0)
    return pl.pallas_call(
        _in_cd_kernel,
        grid=(nb,),
        in_specs=[pl.BlockSpec((tm, D_MODEL), row), pl.BlockSpec((tm, D_MODEL), row), pl.BlockSpec((tm, D_MODEL), row2),
                  pl.BlockSpec((tm, LANES), row), pl.BlockSpec((1, D_MODEL), const),
                  pl.BlockSpec(w.shape, const), pl.BlockSpec((1, C_Q_LORA), const), pl.BlockSpec((1, C_KV_LORA), const),
                  pl.BlockSpec(wuq.shape, const), pl.BlockSpec(wuk.shape, lambda i: (0, 0, 0)),
                  pl.BlockSpec((tm, LANES), row), pl.BlockSpec((tm, LANES), row)],
        out_specs=[pl.BlockSpec((tm, D_MODEL), row), pl.BlockSpec((tm, C_HEADS * C_QK), row),
                   pl.BlockSpec((tm, C_KV_LORA), row), pl.BlockSpec((tm, C_ROPE_DIM), row),
                   pl.BlockSpec((tm, C_QK), row), pl.BlockSpec((tm, D_CONV), row)],
        out_shape=[jax.ShapeDtypeStruct((nt, D_MODEL), F32), jax.ShapeDtypeStruct((nt, C_HEADS * C_QK), F32),
                   jax.ShapeDtypeStruct((nt, C_KV_LORA), F32), jax.ShapeDtypeStruct((nt, C_ROPE_DIM), F32),
                   jax.ShapeDtypeStruct((nt, C_QK), BF16), jax.ShapeDtypeStruct((nt, D_CONV), F32)],
        compiler_params=_cparams(("parallel",), 56),
        name="in_proj_cd",
    )(x, o1, o2, rf, g, w, qn, kvn, wuq, wuk, cos, sin)


def _out_proj_kernel(x_ref, ap_ref, as_ref, bp_ref, bs_ref, w_ref, g_ref, rwh_ref, rwl_ref, rb_ref,
                     xo_ref, hn_ref, ri_ref, rf_ref, cnt_ref, carry_ref, *, n_prompt_tiles):
    tm = x_ref.shape[0]
    step = pl.program_id(0)

    @pl.when(step == 0)
    def _():
        carry_ref[...] = jnp.zeros_like(carry_ref)

    is_prompt = step < n_prompt_tiles
    a = jnp.where(is_prompt, ap_ref[...], as_ref[...])
    b = jnp.where(is_prompt, bp_ref[...], bs_ref[...])
    ab = jnp.concatenate([a, b], axis=1).astype(BF16)
    x = x_ref[...] + _dot(ab, w_ref[...])
    xo_ref[...] = x
    hn = _rms(x, g_ref[...])
    hn_ref[...] = hn
    hh = hn.astype(BF16)
    hl = (hn - hh.astype(F32)).astype(BF16)
    logit = _dot(hh, rwh_ref[...]) + _dot(hl, rwh_ref[...]) + _dot(hh, rwl_ref[...]) + rb_ref[...]
    lane = lax.broadcasted_iota(I32, (tm, LANES), 1)
    lanef = lane.astype(F32)
    big = 1e9
    is_g = (lane >= N_EXPERTS) & (lane < N_EXPERTS + N_GROUPS)
    gl = jnp.where(is_g, logit, NEG_INF)
    gmax = jnp.max(gl, axis=1, keepdims=True)
    grp_lane = jnp.min(jnp.where(gl == gmax, lanef, big), axis=1, keepdims=True)
    g_w = 1.0 / jnp.sum(jnp.where(is_g, jnp.exp(gl - gmax), 0.0), axis=1, keepdims=True)
    grp = grp_lane.astype(I32) - N_EXPERTS
    in_grp = (lane < N_EXPERTS) & (jnp.right_shift(lane, 3) == grp)
    el = jnp.where(in_grp, logit, NEG_INF)
    v1 = jnp.max(el, axis=1, keepdims=True)
    i1 = jnp.min(jnp.where(el == v1, lanef, big), axis=1, keepdims=True)
    el2 = jnp.where(lanef == i1, NEG_INF, el)
    v2 = jnp.max(el2, axis=1, keepdims=True)
    i2 = jnp.min(jnp.where(el2 == v2, lanef, big), axis=1, keepdims=True)
    e21 = jnp.exp(v2 - v1)
    gate1 = g_w / (1.0 + e21)
    gate2 = gate1 * e21
    oh1 = lanef == i1
    oh2 = lanef == i2
    cf = jnp.where(oh1 | oh2, 1.0, 0.0)
    r_i = lax.broadcasted_iota(I32, (tm, tm), 0)
    c_i = lax.broadcasted_iota(I32, (tm, tm), 1)
    tri = jnp.where(r_i > c_i, 1.0, 0.0).astype(BF16)
    carry = carry_ref[...]
    cum = _dot(tri, cf.astype(BF16)) + carry[0:1, :]
    rank1 = jnp.sum(jnp.where(oh1, cum, 0.0), axis=1, keepdims=True)
    rank2 = jnp.sum(jnp.where(oh2, cum, 0.0), axis=1, keepdims=True)
    carry = carry + jnp.sum(cf, axis=0, keepdims=True)
    carry_ref[...] = carry
    cnt_ref[...] = carry
    ri = jnp.where(lane == 0, i1, jnp.where(lane == 1, i2, jnp.where(lane == 2, rank1, jnp.where(lane == 3, rank2, 0.0))))
    ri_ref[...] = ri.astype(I32)
    rf_ref[...] = jnp.where(lane == 0, gate1, jnp.where(lane == 1, gate2, 0.0))


def _out_proj(x, a_p, a_s, b_p, b_s, w, g, rwh, rwl, rb, tm):
    nt = x.shape[0]
    nbp = a_p.shape[0] // tm
    row = lambda i: (i, 0)
    prow = lambda i: (jnp.minimum(i, nbp - 1), 0)
    srow = lambda i: (jnp.maximum(i - nbp, 0), 0)
    const = lambda i: (0, 0)
    return pl.pallas_call(
        functools.partial(_out_proj_kernel, n_prompt_tiles=nbp),
        grid=(nt // tm,),
        in_specs=[pl.BlockSpec((tm, D_MODEL), row), pl.BlockSpec((tm, a_p.shape[1]), prow),
                  pl.BlockSpec((tm, a_s.shape[1]), srow), pl.BlockSpec((tm, b_p.shape[1]), prow),
                  pl.BlockSpec((tm, b_s.shape[1]), srow),
                  pl.BlockSpec(w.shape, const), pl.BlockSpec((1, D_MODEL), const),
                  pl.BlockSpec((D_MODEL, LANES), const), pl.BlockSpec((D_MODEL, LANES), const), pl.BlockSpec((1, LANES), const)],
        out_specs=[pl.BlockSpec((tm, D_MODEL), row), pl.BlockSpec((tm, D_MODEL), row),
                   pl.BlockSpec((tm, LANES), row), pl.BlockSpec((tm, LANES), row), pl.BlockSpec((8, LANES), const)],
        out_shape=[jax.ShapeDtypeStruct((nt, D_MODEL), F32), jax.ShapeDtypeStruct((nt, D_MODEL), F32),
                   jax.ShapeDtypeStruct((nt, LANES), I32), jax.ShapeDtypeStruct((nt, LANES), F32),
                   jax.ShapeDtypeStruct((8, LANES), F32)],
        scratch_shapes=[pltpu.VMEM((8, LANES), F32)],
        compiler_params=_cparams(("arbitrary",), 48),
        name="out_proj_router",
    )(x, a_p, a_s, b_p, b_s, w, g, rwh, rwl, rb)


def _final_norm_kernel(x_ref, o1_ref, o2_ref, rf_ref, g_ref, y_ref):
    y_ref[...] = _rms(_moe_combine(x_ref, o1_ref, o2_ref, rf_ref), g_ref[...])


def _final_norm(x, o, rf, g, tm):
    nt = x.shape[0]
    nb = nt // tm
    row = lambda i: (i, 0)
    return pl.pallas_call(
        _final_norm_kernel,
        grid=(nb,),
        in_specs=[pl.BlockSpec((tm, D_MODEL), row), pl.BlockSpec((tm, D_MODEL), row),
                  pl.BlockSpec((tm, D_MODEL), lambda i: (i + nb, 0)), pl.BlockSpec((tm, LANES), row),
                  pl.BlockSpec((1, D_MODEL), lambda i: (0, 0))],
        out_specs=pl.BlockSpec((tm, D_MODEL), row),
        out_shape=jax.ShapeDtypeStruct((nt, D_MODEL), F32),
        compiler_params=_cparams(("parallel",), 32),
        name="final_norm",
    )(x, o, o, rf, g)


def _moe_kernel(te_ref, nu_ref, tv_ref, rid_ref, did_ref, hn_ref, wg_ref, wu_ref, wd_ref, o_ref,
                xbuf, ybuf, gsem, ssem):
    i = pl.program_id(0)
    n_used = nu_ref[0]
    tmm = ybuf.shape[0]

    def gather(tile, slot):
        base = tile * tmm

        def body(r, c):
            rid = rid_ref[base + r]
            pltpu.make_async_copy(hn_ref.at[pl.ds(rid, 1), :], xbuf.at[slot, pl.ds(r, 1), :], gsem.at[slot]).start()
            return c

        lax.fori_loop(0, tmm, body, 0)

    def gather_wait(slot):
        pltpu.make_async_copy(hn_ref.at[pl.ds(0, tmm), :], xbuf.at[slot], gsem.at[slot]).wait()

    def scatter_wait(tile):
        n_valid = tv_ref[tile]

        @pl.when(n_valid == tmm)
        def _():
            pltpu.make_async_copy(ybuf, o_ref.at[pl.ds(0, tmm), :], ssem.at[0]).wait()

        @pl.when(n_valid < tmm)
        def _():
            def body(r, c):
                pltpu.make_async_copy(ybuf.at[pl.ds(0, 1), :], o_ref.at[pl.ds(0, 1), :], ssem.at[0]).wait()
                return c

            lax.fori_loop(0, n_valid, body, 0)

    @pl.when(i == 0)
    def _():
        gather(0, 0)

    @pl.when(i < n_used)
    def _():
        slot = i % 2

        @pl.when(i + 1 < n_used)
        def _():
            gather(i + 1, 1 - slot)

        gather_wait(slot)
        x = xbuf[slot].astype(BF16)
        hid = jax.nn.silu(_dot(x, wg_ref[...].astype(BF16))) * _dot(x, wu_ref[...].astype(BF16))
        y = _dot(hid.astype(BF16), wd_ref[...].astype(BF16))

        @pl.when(i > 0)
        def _():
            scatter_wait(i - 1)

        ybuf[...] = y
        base = i * tmm

        def body(r, c):
            did = did_ref[base + r]
            pltpu.make_async_copy(ybuf.at[pl.ds(r, 1), :], o_ref.at[pl.ds(did, 1), :], ssem.at[0]).start()
            return c

        lax.fori_loop(0, tv_ref[i], body, 0)

        @pl.when(i == n_used - 1)
        def _():
            scatter_wait(i)


def _moe(hn, tile_expert, n_used, tile_valid, row_ids, dst_ids, w_gate, w_up, w_down, layer):
    nt = hn.shape[0]
    n_tiles = tile_expert.shape[0]
    wspec = lambda shp: pl.BlockSpec((None, None) + shp, lambda i, te, nu, tv, rid, did: (layer, te[i], 0, 0))
    grid_spec = pltpu.PrefetchScalarGridSpec(
        num_scalar_prefetch=5,
        grid=(n_tiles,),
        in_specs=[pl.BlockSpec(memory_space=pl.ANY), wspec((D_MODEL, D_EXPERT)), wspec((D_MODEL, D_EXPERT)),
                  wspec((D_EXPERT, D_MODEL))],
        out_specs=pl.BlockSpec(memory_space=pl.ANY),
        scratch_shapes=[pltpu.VMEM((2, MOE_TILE, D_MODEL), F32), pltpu.VMEM((MOE_TILE, D_MODEL), F32),
                        pltpu.SemaphoreType.DMA((2,)), pltpu.SemaphoreType.DMA((1,))],
    )
    return pl.pallas_call(
        _moe_kernel,
        grid_spec=grid_spec,
        out_shape=jax.ShapeDtypeStruct((2 * nt, D_MODEL), F32),
        compiler_params=_cparams(("arbitrary",), 56),
        name="moe_experts",
    )(tile_expert, n_used, tile_valid, row_ids, dst_ids, hn, w_gate, w_up, w_down)


def _moe_schedule(ri, cnt, nt):
    e1, e2, r1, r2 = ri[:, 0], ri[:, 1], ri[:, 2], ri[:, 3]
    counts = cnt[0, :N_EXPERTS].astype(I32)
    padded = ((counts + MOE_TILE - 1) // MOE_TILE) * MOE_TILE
    ends = jnp.cumsum(padded)
    offs = ends - padded
    n_tiles = -(-2 * nt // MOE_TILE) + N_EXPERTS
    n_used = (ends[-1] // MOE_TILE).astype(I32)
    tile_raw = jnp.arange(n_tiles, dtype=I32)
    tile = jnp.minimum(tile_raw, n_used - 1)
    tile_expert = jnp.minimum(jnp.sum((ends // MOE_TILE)[None, :] <= tile[:, None], axis=1), N_EXPERTS - 1).astype(I32)
    rows_left = counts[tile_expert] - (tile * MOE_TILE - offs[tile_expert])
    tile_valid = jnp.where(tile_raw < n_used, jnp.clip(rows_left, 0, MOE_TILE), 0).astype(I32)
    pos1 = offs[e1] + r1
    pos2 = offs[e2] + r2
    tok = jnp.arange(nt, dtype=I32)
    n_sorted = n_tiles * MOE_TILE
    pos = jnp.concatenate([pos1, pos2])
    row_ids = jnp.zeros((n_sorted,), I32).at[pos].set(jnp.concatenate([tok, tok]))
    dst_ids = jnp.zeros((n_sorted,), I32).at[pos].set(jnp.concatenate([tok, tok + nt]))
    return tile_expert, n_used.reshape(1), tile_valid, row_ids, dst_ids


def _sb_block(q, kt, vt, uo, acc, mask):
    n = len(kt)
    rows = q.shape[0] // n
    z = jnp.concatenate([_dot_nt(q[h * rows:(h + 1) * rows], kt[h]) for h in range(n)], axis=0)
    sp = _softplus(z)
    lk = -sp
    if mask is not None:
        lk = jnp.where(mask, lk, 0.0)
    hi = lk.astype(BF16)
    lo = (lk - hi.astype(F32)).astype(BF16)
    cs = _dot(jnp.concatenate([hi, lo], axis=1), uo)
    w = jnp.exp(z - sp + cs[:, 0:QB] + acc)
    if mask is not None:
        w = jnp.where(mask, w, 0.0)
    w = w.astype(BF16)
    out = jnp.concatenate([_dot(w[h * rows:(h + 1) * rows], vt[h]) for h in range(n)], axis=0)
    return out, acc + cs[:, QB:]


def _sb_prompt_kernel(q_ref, k_ref, v_ref, uo_ref, o_ref, acc_ref, out_ref):
    i = pl.program_id(2)
    q2 = jnp.concatenate([q_ref[:, 0:QB], q_ref[:, QB:2 * QB]], axis=0).astype(BF16)
    uo = uo_ref[...]
    t_i = lax.broadcasted_iota(I32, (2 * QB, QB), 0) % QB
    s_i = lax.broadcasted_iota(I32, (2 * QB, QB), 1)

    def step(j, mask):
        start = pl.multiple_of(j * QB, QB)
        kt = k_ref[pl.ds(start, QB), :].astype(BF16)
        vt = v_ref[pl.ds(start, QB), :].astype(BF16)
        return _sb_block(q2, [kt], [vt], uo, acc_ref[...], mask)

    acc_ref[...] = jnp.zeros_like(acc_ref)
    out, acc = step(i, s_i < t_i)
    out_ref[...] = out
    acc_ref[...] = acc

    def body(jj, c):
        out, acc = step(i - 1 - jj, None)
        out_ref[...] += out
        acc_ref[...] = acc
        return c

    lax.fori_loop(0, i, body, 0)
    o_ref[:, 0:QB] = out_ref[0:QB, :]
    o_ref[:, QB:2 * QB] = out_ref[QB:2 * QB, :]


def _sb_prompt(q, k, v, uo, nb, tp):
    nt = nb * tp
    nq = tp // QB
    return pl.pallas_call(
        _sb_prompt_kernel,
        grid=(nb, A_KV_HEADS, nq),
        in_specs=[pl.BlockSpec((QB, 2 * QB), lambda b, h, i: (b * nq + i, h)),
                  pl.BlockSpec((tp, A_HEAD_DIM), lambda b, h, i: (b, h)),
                  pl.BlockSpec((tp, A_HEAD_DIM), lambda b, h, i: (b, h)),
                  pl.BlockSpec((2 * QB, 2 * QB), lambda b, h, i: (0, 0))],
        out_specs=pl.BlockSpec((QB, 2 * QB), lambda b, h, i: (b * nq + i, h)),
        out_shape=jax.ShapeDtypeStruct((nt, A_WIDTH), F32),
        scratch_shapes=[pltpu.VMEM((2 * QB, QB), F32), pltpu.VMEM((2 * QB, QB), F32)],
        compiler_params=_cparams(("parallel", "parallel", "arbitrary"), 32),
        name="sb_prompt",
    )(q, k, v, uo)


def _sb_sample_kernel(pt_ref, q_ref, kn_ref, vn_ref, *rest, n_grp):
    k_refs = rest[:n_grp]
    v_refs = rest[n_grp:2 * n_grp]
    uo_ref, o_ref, acc_ref, out_ref = rest[2 * n_grp:]
    s = pl.program_id(1)
    ts = q_ref.shape[0]
    rows = 2 * ts
    q = q_ref[...]
    q64 = jnp.concatenate([q[:, h * A_HEAD_DIM:(h + 1) * A_HEAD_DIM] for h in range(A_HEADS)], axis=0).astype(BF16)
    uo = uo_ref[...]

    def heads(x):
        return [x[:, h * A_HEAD_DIM:(h + 1) * A_HEAD_DIM].astype(BF16) for h in range(A_KV_HEADS)]

    @pl.when(s == 0)
    def _():
        pad = jnp.zeros((QB - ts, A_KV_WIDTH), F32)
        kn = jnp.concatenate([kn_ref[...], pad], axis=0)
        vn = jnp.concatenate([vn_ref[...], pad], axis=0)
        t_i = lax.broadcasted_iota(I32, (A_KV_HEADS * rows, QB), 0) % ts
        s_i = lax.broadcasted_iota(I32, (A_KV_HEADS * rows, QB), 1)
        out, acc = _sb_block(q64, heads(kn), heads(vn), uo, jnp.zeros((A_KV_HEADS * rows, QB), F32), s_i < t_i)
        out_ref[...] = out
        acc_ref[...] = acc

    out = out_ref[...]
    acc = acc_ref[...]
    for g in range(n_grp):
        o_g, acc = _sb_block(q64, heads(k_refs[g][...]), heads(v_refs[g][...]), uo, acc, None)
        out = out + o_g
    out_ref[...] = out
    acc_ref[...] = acc

    @pl.when(s == pl.num_programs(1) - 1)
    def _():
        for h in range(A_HEADS):
            o_ref[:, h * A_HEAD_DIM:(h + 1) * A_HEAD_DIM] = out[h * ts:(h + 1) * ts, :]


def _sb_sample(pt_flat, q, k, v, cache_k, cache_v, uo, n_prompt_rows, dec_b, ts, n_pages, n_grp):
    row0 = n_prompt_rows // ts
    newrow = lambda b, s, pt: (row0 + b, 0)

    def page_spec(g):
        return pl.BlockSpec((None, PAGE, A_KV_WIDTH),
                            lambda b, s, pt, g=g: (pt[b * n_pages + n_pages - 1 - (s * n_grp + g)], 0, 0))

    grid_spec = pltpu.PrefetchScalarGridSpec(
        num_scalar_prefetch=1,
        grid=(dec_b, n_pages // n_grp),
        in_specs=[pl.BlockSpec((ts, A_WIDTH), newrow), pl.BlockSpec((ts, A_KV_WIDTH), newrow),
                  pl.BlockSpec((ts, A_KV_WIDTH), newrow)]
                 + [page_spec(g) for g in range(n_grp)] + [page_spec(g) for g in range(n_grp)]
                 + [pl.BlockSpec((2 * QB, 2 * QB), lambda b, s, pt: (0, 0))],
        out_specs=pl.BlockSpec((ts, A_WIDTH), lambda b, s, pt: (b, 0)),
        scratch_shapes=[pltpu.VMEM((A_HEADS * ts, QB), F32), pltpu.VMEM((A_HEADS * ts, QB), F32)],
    )
    return pl.pallas_call(
        functools.partial(_sb_sample_kernel, n_grp=n_grp),
        grid_spec=grid_spec,
        out_shape=jax.ShapeDtypeStruct((dec_b * ts, A_WIDTH), F32),
        compiler_params=_cparams(("parallel", "arbitrary"), 32),
        name="sb_sample",
    )(pt_flat, q, k, v, *([cache_k] * n_grp), *([cache_v] * n_grp), uo)


def _pool_project(d, gi, wp_ref, ps_ref):
    c0 = gi * B_GROUP_WIDTH
    return _dot(d.astype(BF16), wp_ref[gi]) * ps_ref[:, c0:c0 + B_GROUP_WIDTH]


def _pool_prompt_kernel(u_ref, halo_ref, wp_ref, ps_ref, o_ref, ext_ref):
    i = pl.program_id(1)
    tb = u_ref.shape[0]
    ext_ref[0:POOL_MAX, :] = jnp.where(i > 0, halo_ref[...], 0.0)
    ext_ref[POOL_MAX:, :] = u_ref[...]
    pos = (i * tb + lax.broadcasted_iota(I32, (tb, 1), 0)).astype(F32)
    for gi, win in enumerate(POOL_WINDOWS):
        c0 = gi * B_GROUP_WIDTH
        c1 = c0 + B_GROUP_WIDTH
        cur = ext_ref[POOL_MAX:, c0:c1]
        ws = cur
        for k in range(1, win):
            ws = ws + ext_ref[POOL_MAX - k:POOL_MAX - k + tb, c0:c1]
        d = ws * (1.0 / jnp.minimum(pos + 1.0, float(win))) - cur
        o_ref[:, c0:c1] = _pool_project(d, gi, wp_ref, ps_ref)


def _pool_prompt(u, wp, ps, nb, tp):
    nt = nb * tp
    tb = QB
    nq = tp // tb
    hb = tb // POOL_MAX
    return pl.pallas_call(
        _pool_prompt_kernel,
        grid=(nb, nq),
        in_specs=[pl.BlockSpec((tb, B_WIDTH), lambda b, i: (b * nq + i, 0)),
                  pl.BlockSpec((POOL_MAX, B_WIDTH), lambda b, i: (jnp.maximum((b * nq + i) * hb - 1, 0), 0)),
                  pl.BlockSpec(wp.shape, lambda b, i: (0, 0, 0)), pl.BlockSpec((1, B_WIDTH), lambda b, i: (0, 0))],
        out_specs=pl.BlockSpec((tb, B_WIDTH), lambda b, i: (b * nq + i, 0)),
        out_shape=jax.ShapeDtypeStruct((nt, B_WIDTH), F32),
        scratch_shapes=[pltpu.VMEM((POOL_MAX + tb, B_WIDTH), F32)],
        compiler_params=_cparams(("parallel", "parallel"), 32),
        name="pool_prompt",
    )(u, u, wp, ps)


def _pool_sample_kernel(ext_ref, wp_ref, ps_ref, o_ref):
    bb, n_ext, _ = ext_ref.shape
    ts = n_ext - (POOL_MAX - 1)
    for gi, win in enumerate(POOL_WINDOWS):
        c0 = gi * B_GROUP_WIDTH
        c1 = c0 + B_GROUP_WIDTH
        cur = ext_ref[:, POOL_MAX - 1:, c0:c1]
        ws = cur
        for k in range(1, win):
            ws = ws + ext_ref[:, POOL_MAX - 1 - k:POOL_MAX - 1 - k + ts, c0:c1]
        d = (ws * (1.0 / win) - cur).reshape(bb * ts, B_GROUP_WIDTH)
        o_ref[:, c0:c1] = _pool_project(d, gi, wp_ref, ps_ref)


def _pool_sample(ext, wp, ps, bb):
    dec_b, n_ext, _ = ext.shape
    ts = n_ext - (POOL_MAX - 1)
    return pl.pallas_call(
        _pool_sample_kernel,
        grid=(dec_b // bb,),
        in_specs=[pl.BlockSpec((bb, n_ext, B_WIDTH), lambda i: (i, 0, 0)),
                  pl.BlockSpec(wp.shape, lambda i: (0, 0, 0)), pl.BlockSpec((1, B_WIDTH), lambda i: (0, 0))],
        out_specs=pl.BlockSpec((bb * ts, B_WIDTH), lambda i: (i, 0)),
        out_shape=jax.ShapeDtypeStruct((dec_b * ts, B_WIDTH), F32),
        compiler_params=_cparams(("parallel",), 32),
        name="pool_sample",
    )(ext, wp, ps)


def _mla_prompt_kernel(q_ref, kc_ref, wuv_ref, o_ref, m_ref, l_ref, acc_ref):
    i = pl.program_id(1)
    t_i = lax.broadcasted_iota(I32, (QB, QB), 0)
    s_i = lax.broadcasted_iota(I32, (QB, QB), 1)
    m_ref[...] = jnp.full_like(m_ref, NEG_INF)
    l_ref[...] = jnp.zeros_like(l_ref)
    acc_ref[...] = jnp.zeros_like(acc_ref)

    def step(j, mask):
        start = pl.multiple_of(j * QB, QB)
        kc = kc_ref[pl.ds(start, QB), :]
        vt = kc[:, 0:C_KV_LORA]
        for hd in range(C_HEADS):
            q = q_ref[:, hd * C_QK:(hd + 1) * C_QK].astype(BF16)
            s = _dot_nt(q, kc)
            if mask is not None:
                s = jnp.where(mask, s, NEG_INF)
            m_old = m_ref[hd]
            m_new = jnp.maximum(m_old, jnp.max(s, axis=1, keepdims=True))
            alpha = jnp.exp(m_old - m_new)
            p = jnp.exp(s - m_new)
            l_ref[hd] = alpha * l_ref[hd] + jnp.sum(p, axis=1, keepdims=True)
            acc_ref[hd] = alpha * acc_ref[hd] + _dot(p.astype(BF16), vt)
            m_ref[hd] = m_new

    def body(j, c):
        step(j, None)
        return c

    lax.fori_loop(0, i, body, 0)
    step(i, s_i <= t_i)
    for hd in range(C_HEADS):
        o_lat = (acc_ref[hd] / l_ref[hd]).astype(BF16)
        o_ref[:, hd * C_V_DIM:(hd + 1) * C_V_DIM] = _dot(o_lat, wuv_ref[hd])


def _mla_prompt(qc, kc, wuv, nb, tp):
    nt = nb * tp
    nq = tp // QB
    return pl.pallas_call(
        _mla_prompt_kernel,
        grid=(nb, nq),
        in_specs=[pl.BlockSpec((QB, C_HEADS * C_QK), lambda b, i: (b * nq + i, 0)),
                  pl.BlockSpec((tp, C_QK), lambda b, i: (b, 0)),
                  pl.BlockSpec(wuv.shape, lambda b, i: (0, 0, 0))],
        out_specs=pl.BlockSpec((QB, C_WIDTH), lambda b, i: (b * nq + i, 0)),
        out_shape=jax.ShapeDtypeStruct((nt, C_WIDTH), F32),
        scratch_shapes=[pltpu.VMEM((C_HEADS, QB, 1), F32), pltpu.VMEM((C_HEADS, QB, 1), F32),
                        pltpu.VMEM((C_HEADS, QB, C_KV_LORA), F32)],
        compiler_params=_cparams(("parallel", "arbitrary"), 32),
        name="mla_prompt",
    )(qc, kc, wuv)


def _mla_sample_kernel(pt_ref, q_ref, cn_ref, pn_ref, *rest, n_grp):
    c_refs = rest[:n_grp]
    p_refs = rest[n_grp:2 * n_grp]
    wuv_ref, o_ref, m_ref, l_ref, acc_ref = rest[2 * n_grp:]
    s = pl.program_id(1)
    ts = q_ref.shape[0]
    rows = C_HEADS * ts
    q = q_ref[...]
    qh = jnp.concatenate([q[:, h * C_QK:(h + 1) * C_QK] for h in range(C_HEADS)], axis=0)
    q_lat = qh[:, 0:C_KV_LORA].astype(BF16)
    q_pe = qh[:, C_KV_LORA:C_KV_LORA + C_ROPE_DIM].astype(BF16)

    def scores(ckv, kpe):
        return _dot_nt(q_lat, ckv) + _dot_nt(q_pe, kpe)

    def update(ss, vv):
        m_old = m_ref[...]
        m_new = m_old
        for sc in ss:
            m_new = jnp.maximum(m_new, jnp.max(sc, axis=1, keepdims=True))
        alpha = jnp.exp(m_old - m_new)
        l_new = alpha * l_ref[...]
        acc = alpha * acc_ref[...]
        for sc, v in zip(ss, vv):
            p = jnp.exp(sc - m_new)
            l_new = l_new + jnp.sum(p, axis=1, keepdims=True)
            acc = acc + _dot(p.astype(BF16), v)
        m_ref[...] = m_new
        l_ref[...] = l_new
        acc_ref[...] = acc

    @pl.when(s == 0)
    def _():
        m_ref[...] = jnp.full_like(m_ref, NEG_INF)
        l_ref[...] = jnp.zeros_like(l_ref)
        acc_ref[...] = jnp.zeros_like(acc_ref)
        cn = jnp.concatenate([cn_ref[...], jnp.zeros((PAGE - ts, C_KV_LORA), F32)], axis=0).astype(BF16)
        pn = jnp.concatenate([pn_ref[...], jnp.zeros((PAGE - ts, C_ROPE_DIM), F32)], axis=0).astype(BF16)
        t_i = lax.broadcasted_iota(I32, (rows, PAGE), 0) % ts
        s_i = lax.broadcasted_iota(I32, (rows, PAGE), 1)
        update([jnp.where(s_i <= t_i, scores(cn, pn), NEG_INF)], [cn])

    cs = [c_refs[g][...].astype(BF16) for g in range(n_grp)]
    update([scores(cs[g], p_refs[g][...].astype(BF16)) for g in range(n_grp)], cs)

    @pl.when(s == pl.num_programs(1) - 1)
    def _():
        o_lat = (acc_ref[...] / l_ref[...]).astype(BF16)
        for hd in range(C_HEADS):
            o_ref[:, hd * C_V_DIM:(hd + 1) * C_V_DIM] = _dot(o_lat[hd * ts:(hd + 1) * ts], wuv_ref[hd])


def _mla_sample(pt_flat, qc, ckv, kpe, cache_ckv, cache_kpe, wuv, n_prompt_rows, dec_b, ts, n_pages, n_grp):
    row0 = n_prompt_rows // ts
    newrow = lambda b, s, pt: (row0 + b, 0)

    def page_spec(width, g):
        return pl.BlockSpec((None, PAGE, width), lambda b, s, pt, g=g: (pt[b * n_pages + s * n_grp + g], 0, 0))

    grid_spec = pltpu.PrefetchScalarGridSpec(
        num_scalar_prefetch=1,
        grid=(dec_b, n_pages // n_grp),
        in_specs=[pl.BlockSpec((ts, C_HEADS * C_QK), newrow), pl.BlockSpec((ts, C_KV_LORA), newrow),
                  pl.BlockSpec((ts, C_ROPE_DIM), newrow)]
                 + [page_spec(C_KV_LORA, g) for g in range(n_grp)] + [page_spec(C_ROPE_DIM, g) for g in range(n_grp)]
                 + [pl.BlockSpec(wuv.shape, lambda b, s, pt: (0, 0, 0))],
        out_specs=pl.BlockSpec((ts, C_WIDTH), lambda b, s, pt: (b, 0)),
        scratch_shapes=[pltpu.VMEM((C_HEADS * ts, 1), F32), pltpu.VMEM((C_HEADS * ts, 1), F32),
                        pltpu.VMEM((C_HEADS * ts, C_KV_LORA), F32)],
    )
    return pl.pallas_call(
        functools.partial(_mla_sample_kernel, n_grp=n_grp),
        grid_spec=grid_spec,
        out_shape=jax.ShapeDtypeStruct((dec_b * ts, C_WIDTH), F32),
        compiler_params=_cparams(("parallel", "arbitrary"), 32),
        name="mla_sample",
    )(pt_flat, qc, ckv, kpe, *([cache_ckv] * n_grp), *([cache_kpe] * n_grp), wuv)


def _ln_silu(y, g, b):
    yc = y - jnp.mean(y, axis=-1, keepdims=True)
    yn = yc * lax.rsqrt(jnp.mean(yc * yc, axis=-1, keepdims=True) + LN_EPS) * g + b
    return yn * jax.nn.sigmoid(yn)


def _conv_prompt_kernel(u_ref, halo_ref, cw_ref, cb_ref, g_ref, b_ref, o_ref, ext_ref, y_ref):
    i = pl.program_id(1)
    tb = u_ref.shape[0]
    ext_ref[0:CONV_HALO, :] = jnp.where(i > 0, halo_ref[...], 0.0)
    ext_ref[CONV_HALO:, :] = u_ref[...]
    off = CONV_HALO - (CONV_W - 1)
    cw = 256
    for c0 in range(0, D_CONV, cw):
        acc = jnp.zeros((tb, cw), F32) + cb_ref[:, c0:c0 + cw]
        for k in range(CONV_W):
            acc = acc + ext_ref[off + k:off + k + tb, c0:c0 + cw] * cw_ref[k:k + 1, c0:c0 + cw]
        y_ref[:, c0:c0 + cw] = acc
    o_ref[...] = _ln_silu(y_ref[...], g_ref[...], b_ref[...])


def _conv_prompt(u, cw, cb, g, b, nb, tp):
    nt = nb * tp
    tb = QB
    nq = tp // tb
    hb = tb // CONV_HALO
    const = lambda bb, i: (0, 0)
    return pl.pallas_call(
        _conv_prompt_kernel,
        grid=(nb, nq),
        in_specs=[pl.BlockSpec((tb, D_CONV), lambda bb, i: (bb * nq + i, 0)),
                  pl.BlockSpec((CONV_HALO, D_CONV), lambda bb, i: (jnp.maximum((bb * nq + i) * hb - 1, 0), 0)),
                  pl.BlockSpec(cw.shape, const), pl.BlockSpec((1, D_CONV), const), pl.BlockSpec((1, D_CONV), const),
                  pl.BlockSpec((1, D_CONV), const)],
        out_specs=pl.BlockSpec((tb, D_CONV), lambda bb, i: (bb * nq + i, 0)),
        out_shape=jax.ShapeDtypeStruct((nt, D_CONV), F32),
        scratch_shapes=[pltpu.VMEM((CONV_HALO + tb, D_CONV), F32), pltpu.VMEM((tb, D_CONV), F32)],
        compiler_params=_cparams(("parallel", "parallel"), 32),
        name="conv_prompt",
    )(u, u, cw, cb, g, b)


def _conv_sample_kernel(ext_ref, cw_ref, cb_ref, g_ref, b_ref, o_ref):
    bb, n_ext, _ = ext_ref.shape
    ts = n_ext - (CONV_W - 1)
    acc = jnp.zeros((bb, ts, D_CONV), F32) + cb_ref[...]
    for k in range(CONV_W):
        acc = acc + ext_ref[:, k:k + ts, :] * cw_ref[k:k + 1, :]
    o_ref[...] = _ln_silu(acc.reshape(bb * ts, D_CONV), g_ref[...], b_ref[...])


def _conv_sample(ext, cw, cb, g, b, bb):
    dec_b, n_ext, _ = ext.shape
    ts = n_ext - (CONV_W - 1)
    const = lambda i: (0, 0)
    return pl.pallas_call(
        _conv_sample_kernel,
        grid=(dec_b // bb,),
        in_specs=[pl.BlockSpec((bb, n_ext, D_CONV), lambda i: (i, 0, 0)),
                  pl.BlockSpec(cw.shape, const), pl.BlockSpec((1, D_CONV), const), pl.BlockSpec((1, D_CONV), const),
                  pl.BlockSpec((1, D_CONV), const)],
        out_specs=pl.BlockSpec((bb * ts, D_CONV), lambda i: (i, 0)),
        out_shape=jax.ShapeDtypeStruct((dec_b * ts, D_CONV), F32),
        compiler_params=_cparams(("parallel",), 32),
        name="conv_sample",
    )(ext, cw, cb, g, b)


def _router_weights(router_group, router_group_bias, router_expert, router_expert_bias):
    pad = LANES - N_EXPERTS - N_GROUPS
    w = jnp.concatenate([router_expert, router_group, jnp.zeros((D_MODEL, pad), F32)], axis=1)
    b = jnp.concatenate([router_expert_bias, router_group_bias, jnp.zeros((pad,), F32)]).reshape(1, LANES)
    wh = w.astype(BF16)
    wl = (w - wh.astype(F32)).astype(BF16)
    return wh, wl, b


def _cumsum_matrix():
    j = jnp.arange(2 * QB)[:, None] % QB
    c = jnp.arange(2 * QB)[None, :]
    return jnp.where((c >= QB) | (j > c), 1.0, 0.0).astype(BF16)


def kernel(x_prompt, x_sample, cache_sb_k, cache_sb_v, state_pool, cache_mla_ckv, cache_mla_kpe, state_conv, page_table, meta_tokens, norm_mix, norm_ffn, norm_final, w_in_ab, w_pool, pool_scale, w_out_ab, w_in_cd, mla_q_norm, mla_kv_norm, w_uq, w_uk, w_uv, conv_w, conv_b, conv_norm_g, conv_norm_b, w_out_cd, router_group, router_group_bias, router_expert, router_expert_bias, expert_w_gate, expert_w_up, expert_w_down):
    nb, seq, _ = x_prompt.shape
    dec_b, ts, _ = x_sample.shape
    n_pages = page_table.shape[1]
    past_len = n_pages * PAGE
    t_real = seq + N_META
    tp = -(-t_real // QB) * QB
    n_p = nb * tp
    n_s = dec_b * ts
    nt = n_p + n_s
    tm = 256 if nt % 256 == 0 else 128
    assert ts == 8 and n_p % tm == 0 and n_s % tm == 0 and past_len >= POOL_MAX

    meta = jnp.broadcast_to(meta_tokens[None], (nb, N_META, D_MODEL))
    xp = jnp.concatenate([meta, x_prompt, jnp.zeros((nb, tp - t_real, D_MODEL), F32)], axis=1)
    x = jnp.concatenate([xp.reshape(n_p, D_MODEL), x_sample.reshape(n_s, D_MODEL)], axis=0)
    pt_flat = page_table.reshape(-1).astype(I32)
    uo = _cumsum_matrix()
    grp = 8 if n_pages % 8 == 0 else 1
    bb = 16 if dec_b % 16 == 0 else dec_b

    def prompt_rows(a, n_last):
        return a[:n_p].reshape(nb, tp, -1)[:, t_real - n_last:t_real]

    def moe_layer(layer, x_res, a_p, a_s, b_p, b_s, w_out):
        rwh, rwl, rb = _router_weights(router_group[layer], router_group_bias[layer], router_expert[layer],
                                       router_expert_bias[layer])
        x_res, hn, ri, rf, cnt = _out_proj(x_res, a_p, a_s, b_p, b_s, w_out.astype(BF16),
                                           norm_ffn[layer].reshape(1, -1), rwh, rwl, rb, tm)
        sched = _moe_schedule(ri, cnt, nt)
        o = _moe(hn, *sched, expert_w_gate, expert_w_up, expert_w_down, layer)
        return x_res, o, rf

    q, k, v, u = _in_proj_ab(x, norm_mix[0].reshape(1, -1), w_in_ab.astype(BF16), tm)
    oa_p = _sb_prompt(q, k, v, uo, nb, tp)
    oa_s = _sb_sample(pt_flat, q, k, v, cache_sb_k.reshape(-1, PAGE, A_KV_WIDTH),
                      cache_sb_v.reshape(-1, PAGE, A_KV_WIDTH), uo, n_p, dec_b, ts, n_pages, grp)
    wp = w_pool.astype(BF16)
    ps = pool_scale.reshape(1, -1)
    ob_p = _pool_prompt(u, wp, ps, nb, tp)
    pool_ext = jnp.concatenate([state_pool, u[n_p:].reshape(dec_b, ts, B_WIDTH)], axis=1)
    ob_s = _pool_sample(pool_ext, wp, ps, bb)
    x, o, rf = moe_layer(0, x, oa_p, oa_s, ob_p, ob_s, w_out_ab)

    o1 = C_Q_LORA
    o2 = o1 + C_KV_LORA
    o3 = o2 + C_ROPE_DIM
    w_cd = jnp.concatenate([w_in_cd[:, :o2], w_in_cd[:, o3:], w_in_cd[:, o2:o3],
                            jnp.zeros((D_MODEL, LANES - C_ROPE_DIM), F32)], axis=1).astype(BF16)
    wq = w_uq.reshape(C_Q_LORA, C_HEADS, C_NOPE_DIM + C_ROPE_DIM)
    wq_pe = jnp.concatenate([wq[:, :, C_NOPE_DIM:], jnp.zeros((C_Q_LORA, C_HEADS, LANES - C_ROPE_DIM), F32)], axis=2)
    wuq = jnp.concatenate([wq[:, :, :C_NOPE_DIM].reshape(C_Q_LORA, -1), wq_pe.reshape(C_Q_LORA, -1)], axis=1).astype(BF16)
    wuk = jnp.transpose(w_uk, (1, 2, 0)).astype(BF16)
    wuv = jnp.transpose(w_uv, (1, 0, 2)).astype(BF16)
    half = C_ROPE_DIM // 2
    inv_freq = ROPE_THETA ** (-jnp.arange(half, dtype=F32) / half)
    pos = jnp.concatenate([jnp.tile(jnp.arange(tp), nb), jnp.tile(past_len + jnp.arange(ts), dec_b)]).astype(F32)
    ang = pos[:, None] * inv_freq[None, :]
    zpad = jnp.zeros((nt, LANES - C_ROPE_DIM), F32)
    cos = jnp.concatenate([jnp.cos(ang), jnp.cos(ang), zpad], axis=1)
    sin = jnp.concatenate([jnp.sin(ang), jnp.sin(ang), zpad], axis=1)
    x, qc, ckv, kpe, kc, uc = _in_proj_cd(x, o, o, rf, norm_mix[1].reshape(1, -1), w_cd, mla_q_norm.reshape(1, -1),
                                          mla_kv_norm.reshape(1, -1), wuq, wuk, cos, sin, tm)
    oc_p = _mla_prompt(qc, kc, wuv, nb, tp)
    oc_s = _mla_sample(pt_flat, qc, ckv, kpe, cache_mla_ckv, cache_mla_kpe, wuv, n_p, dec_b, ts, n_pages, grp)
    conv_args = (conv_w, conv_b.reshape(1, -1), conv_norm_g.reshape(1, -1), conv_norm_b.reshape(1, -1))
    od_p = _conv_prompt(uc, *conv_args, nb, tp)
    conv_ext = jnp.concatenate([state_conv, uc[n_p:].reshape(dec_b, ts, D_CONV)], axis=1)
    od_s = _conv_sample(conv_ext, *conv_args, bb)
    x, o, rf = moe_layer(1, x, oc_p, oc_s, od_p, od_s, w_out_cd)

    y = _final_norm(x, o, rf, norm_final.reshape(1, -1), tm)

    y_prompt = y[:n_p].reshape(nb, tp, D_MODEL)[:, N_META:t_real]
    y_sample = y[n_p:].reshape(dec_b, ts, D_MODEL)
    kv4 = (A_KV_HEADS, A_HEAD_DIM)
    return (y_prompt, y_sample,
            prompt_rows(k, t_real).reshape(nb, t_real, *kv4), prompt_rows(v, t_real).reshape(nb, t_real, *kv4),
            k[n_p:].reshape(dec_b, ts, *kv4), v[n_p:].reshape(dec_b, ts, *kv4),
            prompt_rows(u, POOL_MAX - 1), pool_ext[:, ts:],
            prompt_rows(ckv, t_real), prompt_rows(kpe, t_real),
            ckv[n_p:].reshape(dec_b, ts, C_KV_LORA), kpe[n_p:].reshape(dec_b, ts, C_ROPE_DIM),
            prompt_rows(uc, CONV_W - 1), conv_ext[:, ts:])
```

```python
import functools

import jax
import jax.numpy as jnp
from jax import lax
from jax.experimental import pallas as pl
from jax.experimental.pallas import tpu as pltpu

F32 = jnp.float32
BF16 = jnp.bfloat16
I32 = jnp.int32

D_MODEL = 2048
N_META = 16
RMS_EPS = 1e-6
LN_EPS = 1e-5
NEG_INF = -1e30
PAGE = 128
LANES = 128
QB = 128

A_HEADS = 8
A_KV_HEADS = 4
A_HEAD_DIM = 128
A_WIDTH = A_HEADS * A_HEAD_DIM
A_KV_WIDTH = A_KV_HEADS * A_HEAD_DIM
A_SCALE = A_HEAD_DIM ** -0.5

POOL_WINDOWS = (2, 4, 8, 16)
POOL_MAX = 16
B_WIDTH = 1024
B_GROUP_WIDTH = B_WIDTH // len(POOL_WINDOWS)

C_HEADS = 8
C_NOPE_DIM = 128
C_ROPE_DIM = 64
C_V_DIM = 128
C_Q_LORA = 512
C_KV_LORA = 256
C_WIDTH = C_HEADS * C_V_DIM
C_SCALE = (C_NOPE_DIM + C_ROPE_DIM) ** -0.5
ROPE_THETA = 10000.0
C_QK = C_KV_LORA + LANES

D_CONV = 1024
CONV_W = 31
CONV_HALO = 32

N_GROUPS = 4
EXPERTS_PER_GROUP = 8
N_EXPERTS = 32
D_EXPERT = 512
MOE_TILE = 256
MOE_DMA_UNROLL = 8


def _cparams(sem, vmem_mb):
    return pltpu.CompilerParams(dimension_semantics=sem, vmem_limit_bytes=vmem_mb << 20)


def _rms(x, g):
    return x * lax.rsqrt(jnp.mean(x * x, axis=-1, keepdims=True) + RMS_EPS) * g


def _dot(a, b):
    return jnp.dot(a, b, preferred_element_type=F32)


def _dot_nt(a, b):
    return lax.dot_general(a, b, (((1,), (1,)), ((), ())), preferred_element_type=F32)


def _softplus(z):
    return jnp.maximum(z, 0.0) + jnp.log(1.0 + jnp.exp(-jnp.abs(z)))


def _moe_combine(x_ref, o1_ref, o2_ref, rf_ref):
    rf = rf_ref[...]
    return x_ref[...] + rf[:, 0:1] * o1_ref[...] + rf[:, 1:2] * o2_ref[...]


def _in_ab_kernel(x_ref, g_ref, w_ref, q_ref, k_ref, v_ref, u_ref):
    h = _rms(x_ref[...], g_ref[...]).astype(BF16)
    q_ref[...] = _dot(h, w_ref[:, 0:A_WIDTH]) * A_SCALE
    k_ref[...] = _dot(h, w_ref[:, A_WIDTH:A_WIDTH + A_KV_WIDTH])
    v_ref[...] = _dot(h, w_ref[:, A_WIDTH + A_KV_WIDTH:A_WIDTH + 2 * A_KV_WIDTH])
    u_ref[...] = _dot(h, w_ref[:, A_WIDTH + 2 * A_KV_WIDTH:])


def _in_proj_ab(x, g, w, tm):
    nt = x.shape[0]
    n_in = w.shape[1]
    row = lambda i: (i, 0)
    const = lambda i: (0, 0)
    return pl.pallas_call(
        _in_ab_kernel,
        grid=(nt // tm,),
        in_specs=[pl.BlockSpec((tm, D_MODEL), row), pl.BlockSpec((1, D_MODEL), const),
                  pl.BlockSpec((D_MODEL, n_in), const)],
        out_specs=[pl.BlockSpec((tm, A_WIDTH), row), pl.BlockSpec((tm, A_KV_WIDTH), row),
                   pl.BlockSpec((tm, A_KV_WIDTH), row), pl.BlockSpec((tm, B_WIDTH), row)],
        out_shape=[jax.ShapeDtypeStruct((nt, A_WIDTH), F32), jax.ShapeDtypeStruct((nt, A_KV_WIDTH), F32),
                   jax.ShapeDtypeStruct((nt, A_KV_WIDTH), F32), jax.ShapeDtypeStruct((nt, B_WIDTH), F32)],
        compiler_params=_cparams(("parallel",), 48),
        name="in_proj_ab",
    )(x, g, w)


def _rot_half(x, lane_period):
    n = x.shape[-1]
    half = C_ROPE_DIM // 2
    lane = lax.broadcasted_iota(I32, x.shape, x.ndim - 1) % lane_period
    fwd = pltpu.roll(x, n - half, x.ndim - 1)
    bwd = pltpu.roll(x, half, x.ndim - 1)
    return jnp.where(lane < half, -fwd, jnp.where(lane < C_ROPE_DIM, bwd, 0.0))


def _in_cd_kernel(x_ref, o1_ref, o2_ref, rf_ref, g_ref, w_ref, qn_ref, kvn_ref, wuq_ref, wuk_ref, cos_ref, sin_ref,
                  xo_ref, qc_ref, ckv_ref, kpe_ref, kc_ref, u_ref):
    x = _moe_combine(x_ref, o1_ref, o2_ref, rf_ref)
    xo_ref[...] = x
    h = _rms(x, g_ref[...]).astype(BF16)
    cos = cos_ref[...]
    sin = sin_ref[...]
    cq = _rms(_dot(h, w_ref[:, 0:C_Q_LORA]), qn_ref[...]).astype(BF16)
    for hd in range(C_HEADS):
        qn = _dot(cq, wuq_ref[:, hd * LANES:(hd + 1) * LANES]).astype(BF16)
        qc_ref[:, hd * C_QK:hd * C_QK + C_KV_LORA] = _dot(qn, wuk_ref[hd]) * C_SCALE
        qp = _dot(cq, wuq_ref[:, (C_HEADS + hd) * LANES:(C_HEADS + hd + 1) * LANES])
        qc_ref[:, hd * C_QK + C_KV_LORA:(hd + 1) * C_QK] = (qp * cos + _rot_half(qp, LANES) * sin) * C_SCALE
    o1 = C_Q_LORA
    o2 = o1 + C_KV_LORA
    ckv = _rms(_dot(h, w_ref[:, o1:o2]), kvn_ref[...])
    ckv_ref[...] = ckv
    o3 = o2 + 2 * D_CONV
    kr = _dot(h, w_ref[:, o3:o3 + LANES])
    kpe = kr * cos + _rot_half(kr, LANES) * sin
    kpe_ref[...] = kpe[:, 0:C_ROPE_DIM]
    kc_ref[:, 0:C_KV_LORA] = ckv.astype(BF16)
    kc_ref[:, C_KV_LORA:] = kpe.astype(BF16)
    ga = _dot(h, w_ref[:, o2:o2 + D_CONV])
    gb = _dot(h, w_ref[:, o2 + D_CONV:o3])
    u_ref[...] = ga * jax.nn.sigmoid(gb)


def _in_proj_cd(x, o1, o2, rf, g, w, qn, kvn, wuq, wuk, cos, sin, tm):
    nt = x.shape[0]
    nb = nt // tm
    row = lambda i: (i, 0)
    row2 = lambda i: (i + nb, 0)
    const = lambda i: (0, 0)
    return pl.pallas_call(
        _in_cd_kernel,
        grid=(nb,),
        in_specs=[pl.BlockSpec((tm, D_MODEL), row), pl.BlockSpec((tm, D_MODEL), row), pl.BlockSpec((tm, D_MODEL), row2),
                  pl.BlockSpec((tm, LANES), row), pl.BlockSpec((1, D_MODEL), const),
                  pl.BlockSpec(w.shape, const), pl.BlockSpec((1, C_Q_LORA), const), pl.BlockSpec((1, C_KV_LORA), const),
                  pl.BlockSpec(wuq.shape, const), pl.BlockSpec(wuk.shape, lambda i: (0, 0, 0)),
                  pl.BlockSpec((tm, LANES), row), pl.BlockSpec((tm, LANES), row)],
        out_specs=[pl.BlockSpec((tm, D_MODEL), row), pl.BlockSpec((tm, C_HEADS * C_QK), row),
                   pl.BlockSpec((tm, C_KV_LORA), row), pl.BlockSpec((tm, C_ROPE_DIM), row),
                   pl.BlockSpec((tm, C_QK), row), pl.BlockSpec((tm, D_CONV), row)],
        out_shape=[jax.ShapeDtypeStruct((nt, D_MODEL), F32), jax.ShapeDtypeStruct((nt, C_HEADS * C_QK), F32),
                   jax.ShapeDtypeStruct((nt, C_KV_LORA), F32), jax.ShapeDtypeStruct((nt, C_ROPE_DIM), F32),
                   jax.ShapeDtypeStruct((nt, C_QK), BF16), jax.ShapeDtypeStruct((nt, D_CONV), F32)],
        compiler_params=_cparams(("parallel",), 56),
        name="in_proj_cd",
    )(x, o1, o2, rf, g, w, qn, kvn, wuq, wuk, cos, sin)


def _out_proj_kernel(x_ref, ap_ref, as_ref, bp_ref, bs_ref, w_ref, g_ref, rwh_ref, rwl_ref, rb_ref,
                     xo_ref, hn_ref, ri_ref, rf_ref, cnt_ref, carry_ref, *, n_prompt_tiles):
    tm = x_ref.shape[0]
    step = pl.program_id(0)

    @pl.when(step == 0)
    def _():
        carry_ref[...] = jnp.zeros_like(carry_ref)

    is_prompt = step < n_prompt_tiles
    a = jnp.where(is_prompt, ap_ref[...], as_ref[...])
    b = jnp.where(is_prompt, bp_ref[...], bs_ref[...])
    ab = jnp.concatenate([a, b], axis=1).astype(BF16)
    x = x_ref[...] + _dot(ab, w_ref[...])
    xo_ref[...] = x
    hn = _rms(x, g_ref[...])
    hn_ref[...] = hn
    hh = hn.astype(BF16)
    hl = (hn - hh.astype(F32)).astype(BF16)
    logit = _dot(hh, rwh_ref[...]) + _dot(hl, rwh_ref[...]) + _dot(hh, rwl_ref[...]) + rb_ref[...]
    lane = lax.broadcasted_iota(I32, (tm, LANES), 1)
    lanef = lane.astype(F32)
    big = 1e9
    is_g = (lane >= N_EXPERTS) & (lane < N_EXPERTS + N_GROUPS)
    gl = jnp.where(is_g, logit, NEG_INF)
    gmax = jnp.max(gl, axis=1, keepdims=True)
    grp_lane = jnp.min(jnp.where(gl == gmax, lanef, big), axis=1, keepdims=True)
    g_w = 1.0 / jnp.sum(jnp.where(is_g, jnp.exp(gl - gmax), 0.0), axis=1, keepdims=True)
    grp = grp_lane.astype(I32) - N_EXPERTS
    in_grp = (lane < N_EXPERTS) & (jnp.right_shift(lane, 3) == grp)
    el = jnp.where(in_grp, logit, NEG_INF)
    v1 = jnp.max(el, axis=1, keepdims=True)
    i1 = jnp.min(jnp.where(el == v1, lanef, big), axis=1, keepdims=True)
    el2 = jnp.where(lanef == i1, NEG_INF, el)
    v2 = jnp.max(el2, axis=1, keepdims=True)
    i2 = jnp.min(jnp.where(el2 == v2, lanef, big), axis=1, keepdims=True)
    e21 = jnp.exp(v2 - v1)
    gate1 = g_w / (1.0 + e21)
    gate2 = gate1 * e21
    oh1 = lanef == i1
    oh2 = lanef == i2
    cf = jnp.where(oh1 | oh2, 1.0, 0.0)
    r_i = lax.broadcasted_iota(I32, (tm, tm), 0)
    c_i = lax.broadcasted_iota(I32, (tm, tm), 1)
    tri = jnp.where(r_i > c_i, 1.0, 0.0).astype(BF16)
    carry = carry_ref[...]
    cum = _dot(tri, cf.astype(BF16)) + carry[0:1, :]
    rank1 = jnp.sum(jnp.where(oh1, cum, 0.0), axis=1, keepdims=True)
    rank2 = jnp.sum(jnp.where(oh2, cum, 0.0), axis=1, keepdims=True)
    carry = carry + jnp.sum(cf, axis=0, keepdims=True)
    carry_ref[...] = carry
    cnt_ref[...] = carry
    ri = jnp.where(lane == 0, i1, jnp.where(lane == 1, i2, jnp.where(lane == 2, rank1, jnp.where(lane == 3, rank2, 0.0))))
    ri_ref[...] = ri.astype(I32)
    rf_ref[...] = jnp.where(lane == 0, gate1, jnp.where(lane == 1, gate2, 0.0))


def _out_proj(x, a_p, a_s, b_p, b_s, w, g, rwh, rwl, rb, tm):
    nt = x.shape[0]
    nbp = a_p.shape[0] // tm
    row = lambda i: (i, 0)
    prow = lambda i: (jnp.minimum(i, nbp - 1), 0)
    srow = lambda i: (jnp.maximum(i - nbp, 0), 0)
    const = lambda i: (0, 0)
    return pl.pallas_call(
        functools.partial(_out_proj_kernel, n_prompt_tiles=nbp),
        grid=(nt // tm,),
        in_specs=[pl.BlockSpec((tm, D_MODEL), row), pl.BlockSpec((tm, a_p.shape[1]), prow),
                  pl.BlockSpec((tm, a_s.shape[1]), srow), pl.BlockSpec((tm, b_p.shape[1]), prow),
                  pl.BlockSpec((tm, b_s.shape[1]), srow),
                  pl.BlockSpec(w.shape, const), pl.BlockSpec((1, D_MODEL), const),
                  pl.BlockSpec((D_MODEL, LANES), const), pl.BlockSpec((D_MODEL, LANES), const), pl.BlockSpec((1, LANES), const)],
        out_specs=[pl.BlockSpec((tm, D_MODEL), row), pl.BlockSpec((tm, D_MODEL), row),
                   pl.BlockSpec((tm, LANES), row), pl.BlockSpec((tm, LANES), row), pl.BlockSpec((8, LANES), const)],
        out_shape=[jax.ShapeDtypeStruct((nt, D_MODEL), F32), jax.ShapeDtypeStruct((nt, D_MODEL), F32),
                   jax.ShapeDtypeStruct((nt, LANES), I32), jax.ShapeDtypeStruct((nt, LANES), F32),
                   jax.ShapeDtypeStruct((8, LANES), F32)],
        scratch_shapes=[pltpu.VMEM((8, LANES), F32)],
        compiler_params=_cparams(("arbitrary",), 48),
        name="out_proj_router",
    )(x, a_p, a_s, b_p, b_s, w, g, rwh, rwl, rb)


def _final_norm_kernel(x_ref, o1_ref, o2_ref, rf_ref, g_ref, y_ref):
    y_ref[...] = _rms(_moe_combine(x_ref, o1_ref, o2_ref, rf_ref), g_ref[...])


def _final_norm(x, o, rf, g, tm):
    nt = x.shape[0]
    nb = nt // tm
    row = lambda i: (i, 0)
    return pl.pallas_call(
        _final_norm_kernel,
        grid=(nb,),
        in_specs=[pl.BlockSpec((tm, D_MODEL), row), pl.BlockSpec((tm, D_MODEL), row),
                  pl.BlockSpec((tm, D_MODEL), lambda i: (i + nb, 0)), pl.BlockSpec((tm, LANES), row),
                  pl.BlockSpec((1, D_MODEL), lambda i: (0, 0))],
        out_specs=pl.BlockSpec((tm, D_MODEL), row),
        out_shape=jax.ShapeDtypeStruct((nt, D_MODEL), F32),
        compiler_params=_cparams(("parallel",), 32),
        name="final_norm",
    )(x, o, o, rf, g)


def _moe_kernel(te_ref, nu_ref, tv_ref, rid_ref, did_ref, hn_ref, wg_ref, wu_ref, wd_ref, o_ref,
                xbuf, ybuf, gsem, ssem):
    i = pl.program_id(0)
    n_used = nu_ref[0]
    tmm = ybuf.shape[0]

    def gather(tile, slot):
        base = tile * tmm

        def body(c8, c):
            for u in range(MOE_DMA_UNROLL):
                r = c8 * MOE_DMA_UNROLL + u
                rid = rid_ref[base + r]
                pltpu.make_async_copy(hn_ref.at[pl.ds(rid, 1), :], xbuf.at[slot, pl.ds(r, 1), :], gsem.at[slot]).start()
            return c

        lax.fori_loop(0, tmm // MOE_DMA_UNROLL, body, 0)

    def gather_wait(slot):
        pltpu.make_async_copy(hn_ref.at[pl.ds(0, tmm), :], xbuf.at[slot], gsem.at[slot]).wait()

    def scatter_wait(tile):
        n_valid = tv_ref[tile]

        @pl.when(n_valid == tmm)
        def _():
            pltpu.make_async_copy(ybuf, o_ref.at[pl.ds(0, tmm), :], ssem.at[0]).wait()

        @pl.when(n_valid < tmm)
        def _():
            def body(r, c):
                pltpu.make_async_copy(ybuf.at[pl.ds(0, 1), :], o_ref.at[pl.ds(0, 1), :], ssem.at[0]).wait()
                return c

            lax.fori_loop(0, n_valid, body, 0)

    @pl.when(i == 0)
    def _():
        gather(0, 0)

    @pl.when(i < n_used)
    def _():
        slot = i % 2

        @pl.when(i + 1 < n_used)
        def _():
            gather(i + 1, 1 - slot)

        gather_wait(slot)
        x = xbuf[slot].astype(BF16)
        hid = jax.nn.silu(_dot(x, wg_ref[...].astype(BF16))) * _dot(x, wu_ref[...].astype(BF16))
        y = _dot(hid.astype(BF16), wd_ref[...].astype(BF16))

        @pl.when(i > 0)
        def _():
            scatter_wait(i - 1)

        ybuf[...] = y
        base = i * tmm

        def scatter_row(r):
            did = did_ref[base + r]
            pltpu.make_async_copy(ybuf.at[pl.ds(r, 1), :], o_ref.at[pl.ds(did, 1), :], ssem.at[0]).start()

        def body_unrolled(c8, c):
            for u in range(MOE_DMA_UNROLL):
                scatter_row(c8 * MOE_DMA_UNROLL + u)
            return c

        def body_single(r, c):
            scatter_row(r)
            return c

        n_valid = tv_ref[i]
        n_chunks = n_valid // MOE_DMA_UNROLL
        lax.fori_loop(0, n_chunks, body_unrolled, 0)
        lax.fori_loop(n_chunks * MOE_DMA_UNROLL, n_valid, body_single, 0)

        @pl.when(i == n_used - 1)
        def _():
            scatter_wait(i)


def _moe(hn, tile_expert, n_used, tile_valid, row_ids, dst_ids, w_gate, w_up, w_down, layer):
    nt = hn.shape[0]
    n_tiles = tile_expert.shape[0]
    wspec = lambda shp: pl.BlockSpec((None, None) + shp, lambda i, te, nu, tv, rid, did: (layer, te[i], 0, 0))
    grid_spec = pltpu.PrefetchScalarGridSpec(
        num_scalar_prefetch=5,
        grid=(n_tiles,),
        in_specs=[pl.BlockSpec(memory_space=pl.ANY), wspec((D_MODEL, D_EXPERT)), wspec((D_MODEL, D_EXPERT)),
                  wspec((D_EXPERT, D_MODEL))],
        out_specs=pl.BlockSpec(memory_space=pl.ANY),
        scratch_shapes=[pltpu.VMEM((2, MOE_TILE, D_MODEL), F32), pltpu.VMEM((MOE_TILE, D_MODEL), F32),
                        pltpu.SemaphoreType.DMA((2,)), pltpu.SemaphoreType.DMA((1,))],
    )
    return pl.pallas_call(
        _moe_kernel,
        grid_spec=grid_spec,
        out_shape=jax.ShapeDtypeStruct((2 * nt, D_MODEL), F32),
        compiler_params=_cparams(("arbitrary",), 56),
        name="moe_experts",
    )(tile_expert, n_used, tile_valid, row_ids, dst_ids, hn, w_gate, w_up, w_down)


def _moe_schedule(ri, cnt, nt):
    e1, e2, r1, r2 = ri[:, 0], ri[:, 1], ri[:, 2], ri[:, 3]
    counts = cnt[0, :N_EXPERTS].astype(I32)
    padded = ((counts + MOE_TILE - 1) // MOE_TILE) * MOE_TILE
    ends = jnp.cumsum(padded)
    offs = ends - padded
    n_tiles = -(-2 * nt // MOE_TILE) + N_EXPERTS
    n_used = (ends[-1] // MOE_TILE).astype(I32)
    tile_raw = jnp.arange(n_tiles, dtype=I32)
    tile = jnp.minimum(tile_raw, n_used - 1)
    tile_expert = jnp.minimum(jnp.sum((ends // MOE_TILE)[None, :] <= tile[:, None], axis=1), N_EXPERTS - 1).astype(I32)
    rows_left = counts[tile_expert] - (tile * MOE_TILE - offs[tile_expert])
    tile_valid = jnp.where(tile_raw < n_used, jnp.clip(rows_left, 0, MOE_TILE), 0).astype(I32)
    pos1 = offs[e1] + r1
    pos2 = offs[e2] + r2
    tok = jnp.arange(nt, dtype=I32)
    n_sorted = n_tiles * MOE_TILE
    pos = jnp.concatenate([pos1, pos2])
    row_ids = jnp.zeros((n_sorted,), I32).at[pos].set(jnp.concatenate([tok, tok]))
    dst_ids = jnp.zeros((n_sorted,), I32).at[pos].set(jnp.concatenate([tok, tok + nt]))
    return tile_expert, n_used.reshape(1), tile_valid, row_ids, dst_ids


def _sb_blocks(q, kt, vt, uo, acc, mask):
    n_h = len(q)
    rows = q[0].shape[0]
    n_blk = kt[0].shape[0] // QB
    z = jnp.concatenate([_dot_nt(q[h], kt[h]) for h in range(n_h)], axis=0)
    sp = _softplus(z)
    lk = -sp
    if mask is not None:
        lk = jnp.where(mask, lk, 0.0)
    hi = lk.astype(BF16)
    lo = (lk - hi.astype(F32)).astype(BF16)
    split = jnp.concatenate([jnp.concatenate([hi[:, g * QB:(g + 1) * QB], lo[:, g * QB:(g + 1) * QB]], axis=1)
                             for g in range(n_blk)], axis=0)
    cs = _dot(split, uo)
    later = []
    for g in range(n_blk):
        cs_g = cs[g * n_h * rows:(g + 1) * n_h * rows]
        later.append(cs_g[:, 0:QB] + acc)
        acc = acc + cs_g[:, QB:]
    w = jnp.exp(z - sp + jnp.concatenate(later, axis=1))
    if mask is not None:
        w = jnp.where(mask, w, 0.0)
    w = w.astype(BF16)
    out = jnp.concatenate([_dot(w[h * rows:(h + 1) * rows], vt[h]) for h in range(n_h)], axis=0)
    return out, acc


def _sb_prompt_kernel(q_ref, k_ref, v_ref, uo_ref, o_ref, acc_ref, out_ref):
    i = pl.program_id(1)
    group = A_HEADS // A_KV_HEADS
    gw = group * A_HEAD_DIM
    q = [jnp.concatenate([q_ref[:, h * gw + g * A_HEAD_DIM:h * gw + (g + 1) * A_HEAD_DIM] for g in range(group)],
                         axis=0).astype(BF16) for h in range(A_KV_HEADS)]
    uo = uo_ref[...]
    t_i = lax.broadcasted_iota(I32, (A_HEADS * QB, QB), 0) % QB
    s_i = lax.broadcasted_iota(I32, (A_HEADS * QB, QB), 1)

    def step(j, mask):
        start = pl.multiple_of(j * QB, QB)
        kt = [k_ref[pl.ds(start, QB), h * A_HEAD_DIM:(h + 1) * A_HEAD_DIM].astype(BF16) for h in range(A_KV_HEADS)]
        vt = [v_ref[pl.ds(start, QB), h * A_HEAD_DIM:(h + 1) * A_HEAD_DIM].astype(BF16) for h in range(A_KV_HEADS)]
        return _sb_blocks(q, kt, vt, uo, acc_ref[...], mask)

    acc_ref[...] = jnp.zeros_like(acc_ref)
    out, acc = step(i, s_i < t_i)
    out_ref[...] = out
    acc_ref[...] = acc

    def body(jj, c):
        out, acc = step(i - 1 - jj, None)
        out_ref[...] += out
        acc_ref[...] = acc
        return c

    lax.fori_loop(0, i, body, 0)
    for hh in range(A_HEADS):
        o_ref[:, hh * A_HEAD_DIM:(hh + 1) * A_HEAD_DIM] = out_ref[hh * QB:(hh + 1) * QB, :]


def _sb_prompt(q, k, v, uo, nb, tp):
    nt = nb * tp
    nq = tp // QB
    return pl.pallas_call(
        _sb_prompt_kernel,
        grid=(nb, nq),
        in_specs=[pl.BlockSpec((QB, A_WIDTH), lambda b, i: (b * nq + i, 0)),
                  pl.BlockSpec((tp, A_KV_WIDTH), lambda b, i: (b, 0)),
                  pl.BlockSpec((tp, A_KV_WIDTH), lambda b, i: (b, 0)),
                  pl.BlockSpec((2 * QB, 2 * QB), lambda b, i: (0, 0))],
        out_specs=pl.BlockSpec((QB, A_WIDTH), lambda b, i: (b * nq + i, 0)),
        out_shape=jax.ShapeDtypeStruct((nt, A_WIDTH), F32),
        scratch_shapes=[pltpu.VMEM((A_HEADS * QB, QB), F32), pltpu.VMEM((A_HEADS * QB, QB), F32)],
        compiler_params=_cparams(("parallel", "arbitrary"), 40),
        name="sb_prompt",
    )(q, k, v, uo)


def _sb_sample_kernel(pt_ref, q_ref, kn_ref, vn_ref, *rest, n_grp):
    k_refs = rest[:n_grp]
    v_refs = rest[n_grp:2 * n_grp]
    uo_ref, o_ref, acc_ref, out_ref = rest[2 * n_grp:]
    s = pl.program_id(1)
    ts = q_ref.shape[0]
    group = A_HEADS // A_KV_HEADS
    gw = group * A_HEAD_DIM
    qf = q_ref[...]
    q = [jnp.concatenate([qf[:, h * gw + g * A_HEAD_DIM:h * gw + (g + 1) * A_HEAD_DIM] for g in range(group)],
                         axis=0).astype(BF16) for h in range(A_KV_HEADS)]
    uo = uo_ref[...]
    n_rows = A_HEADS * ts

    @pl.when(s == 0)
    def _():
        pad = jnp.zeros((QB - ts, A_KV_WIDTH), F32)
        kn = jnp.concatenate([kn_ref[...], pad], axis=0)
        vn = jnp.concatenate([vn_ref[...], pad], axis=0)
        heads = lambda x: [x[:, h * A_HEAD_DIM:(h + 1) * A_HEAD_DIM].astype(BF16) for h in range(A_KV_HEADS)]
        t_i = lax.broadcasted_iota(I32, (n_rows, QB), 0) % ts
        s_i = lax.broadcasted_iota(I32, (n_rows, QB), 1)
        out, acc = _sb_blocks(q, heads(kn), heads(vn), uo, jnp.zeros((n_rows, QB), F32), s_i < t_i)
        out_ref[...] = out
        acc_ref[...] = acc

    def head_rows(ref, h):
        return ref[pl.ds(h, PAGE, stride=A_KV_HEADS), :]

    kt = [jnp.concatenate([head_rows(k_refs[g], h) for g in range(n_grp)], axis=0).astype(BF16)
          for h in range(A_KV_HEADS)]
    vt = [jnp.concatenate([head_rows(v_refs[g], h) for g in range(n_grp)], axis=0).astype(BF16)
          for h in range(A_KV_HEADS)]
    out, acc = _sb_blocks(q, kt, vt, uo, acc_ref[...], None)
    out = out_ref[...] + out
    out_ref[...] = out
    acc_ref[...] = acc

    @pl.when(s == pl.num_programs(1) - 1)
    def _():
        for h in range(A_HEADS):
            o_ref[:, h * A_HEAD_DIM:(h + 1) * A_HEAD_DIM] = out[h * ts:(h + 1) * ts, :]


def _sb_sample(pt_flat, q, k, v, cache_k, cache_v, uo, n_prompt_rows, dec_b, ts, n_pages, n_grp):
    row0 = n_prompt_rows // ts
    newrow = lambda b, s, pt: (row0 + b, 0)

    def page_spec(g):
        return pl.BlockSpec((None, PAGE * A_KV_HEADS, A_HEAD_DIM),
                            lambda b, s, pt, g=g: (pt[b * n_pages + n_pages - 1 - (s * n_grp + g)], 0, 0))

    grid_spec = pltpu.PrefetchScalarGridSpec(
        num_scalar_prefetch=1,
        grid=(dec_b, n_pages // n_grp),
        in_specs=[pl.BlockSpec((ts, A_WIDTH), newrow), pl.BlockSpec((ts, A_KV_WIDTH), newrow),
                  pl.BlockSpec((ts, A_KV_WIDTH), newrow)]
                 + [page_spec(g) for g in range(n_grp)] + [page_spec(g) for g in range(n_grp)]
                 + [pl.BlockSpec((2 * QB, 2 * QB), lambda b, s, pt: (0, 0))],
        out_specs=pl.BlockSpec((ts, A_WIDTH), lambda b, s, pt: (b, 0)),
        scratch_shapes=[pltpu.VMEM((A_HEADS * ts, QB), F32), pltpu.VMEM((A_HEADS * ts, QB), F32)],
    )
    return pl.pallas_call(
        functools.partial(_sb_sample_kernel, n_grp=n_grp),
        grid_spec=grid_spec,
        out_shape=jax.ShapeDtypeStruct((dec_b * ts, A_WIDTH), F32),
        compiler_params=_cparams(("parallel", "arbitrary"), 32),
        name="sb_sample",
    )(pt_flat, q, k, v, *([cache_k] * n_grp), *([cache_v] * n_grp), uo)


def _pool_project(d, gi, wp_ref, ps_ref):
    c0 = gi * B_GROUP_WIDTH
    return _dot(d.astype(BF16), wp_ref[gi]) * ps_ref[:, c0:c0 + B_GROUP_WIDTH]


def _pool_prompt_kernel(u_ref, halo_ref, wp_ref, ps_ref, o_ref, ext_ref):
    i = pl.program_id(1)
    tb = u_ref.shape[0]
    ext_ref[0:POOL_MAX, :] = jnp.where(i > 0, halo_ref[...], 0.0)
    ext_ref[POOL_MAX:, :] = u_ref[...]
    pos = (i * tb + lax.broadcasted_iota(I32, (tb, 1), 0)).astype(F32)
    for gi, win in enumerate(POOL_WINDOWS):
        c0 = gi * B_GROUP_WIDTH
        c1 = c0 + B_GROUP_WIDTH
        cur = ext_ref[POOL_MAX:, c0:c1]
        ws = cur
        for k in range(1, win):
            ws = ws + ext_ref[POOL_MAX - k:POOL_MAX - k + tb, c0:c1]
        d = ws * (1.0 / jnp.minimum(pos + 1.0, float(win))) - cur
        o_ref[:, c0:c1] = _pool_project(d, gi, wp_ref, ps_ref)


def _pool_prompt(u, wp, ps, nb, tp):
    nt = nb * tp
    tb = QB
    nq = tp // tb
    hb = tb // POOL_MAX
    return pl.pallas_call(
        _pool_prompt_kernel,
        grid=(nb, nq),
        in_specs=[pl.BlockSpec((tb, B_WIDTH), lambda b, i: (b * nq + i, 0)),
                  pl.BlockSpec((POOL_MAX, B_WIDTH), lambda b, i: (jnp.maximum((b * nq + i) * hb - 1, 0), 0)),
                  pl.BlockSpec(wp.shape, lambda b, i: (0, 0, 0)), pl.BlockSpec((1, B_WIDTH), lambda b, i: (0, 0))],
        out_specs=pl.BlockSpec((tb, B_WIDTH), lambda b, i: (b * nq + i, 0)),
        out_shape=jax.ShapeDtypeStruct((nt, B_WIDTH), F32),
        scratch_shapes=[pltpu.VMEM((POOL_MAX + tb, B_WIDTH), F32)],
        compiler_params=_cparams(("parallel", "parallel"), 32),
        name="pool_prompt",
    )(u, u, wp, ps)


def _pool_sample_kernel(ext_ref, wp_ref, ps_ref, o_ref):
    bb, n_ext, _ = ext_ref.shape
    ts = n_ext - (POOL_MAX - 1)
    for gi, win in enumerate(POOL_WINDOWS):
        c0 = gi * B_GROUP_WIDTH
        c1 = c0 + B_GROUP_WIDTH
        cur = ext_ref[:, POOL_MAX - 1:, c0:c1]
        ws = cur
        for k in range(1, win):
            ws = ws + ext_ref[:, POOL_MAX - 1 - k:POOL_MAX - 1 - k + ts, c0:c1]
        d = (ws * (1.0 / win) - cur).reshape(bb * ts, B_GROUP_WIDTH)
        o_ref[:, c0:c1] = _pool_project(d, gi, wp_ref, ps_ref)


def _pool_sample(ext, wp, ps, bb):
    dec_b, n_ext, _ = ext.shape
    ts = n_ext - (POOL_MAX - 1)
    return pl.pallas_call(
        _pool_sample_kernel,
        grid=(dec_b // bb,),
        in_specs=[pl.BlockSpec((bb, n_ext, B_WIDTH), lambda i: (i, 0, 0)),
                  pl.BlockSpec(wp.shape, lambda i: (0, 0, 0)), pl.BlockSpec((1, B_WIDTH), lambda i: (0, 0))],
        out_specs=pl.BlockSpec((bb * ts, B_WIDTH), lambda i: (i, 0)),
        out_shape=jax.ShapeDtypeStruct((dec_b * ts, B_WIDTH), F32),
        compiler_params=_cparams(("parallel",), 32),
        name="pool_sample",
    )(ext, wp, ps)


def _mla_prompt_kernel(q_ref, kc_ref, wuv_ref, o_ref, qs_ref, m_ref, l_ref, acc_ref):
    i = pl.program_id(1)
    n_rows = C_HEADS * QB
    t_i = lax.broadcasted_iota(I32, (n_rows, QB), 0) % QB
    s_i = lax.broadcasted_iota(I32, (n_rows, QB), 1)
    for hd in range(C_HEADS):
        qs_ref[hd * QB:(hd + 1) * QB, :] = q_ref[:, hd * C_QK:(hd + 1) * C_QK].astype(BF16)
    ones = jnp.ones((QB, LANES), BF16)
    m_ref[...] = jnp.full_like(m_ref, NEG_INF)
    l_ref[...] = jnp.zeros_like(l_ref)
    acc_ref[...] = jnp.zeros_like(acc_ref)

    def step(j, mask):
        start = pl.multiple_of(j * QB, QB)
        kc = kc_ref[pl.ds(start, QB), :]
        s = _dot_nt(qs_ref[...], kc)
        if mask is not None:
            s = jnp.where(mask, s, NEG_INF)
        m_old = m_ref[...]
        m_new = jnp.maximum(m_old, jnp.max(s, axis=1, keepdims=True))
        alpha = jnp.exp(m_old - m_new)
        p = jnp.exp(s - m_new).astype(BF16)
        l_ref[...] = alpha * l_ref[...] + _dot(p, ones)
        acc_ref[...] = alpha * acc_ref[...] + _dot(p, kc[:, 0:C_KV_LORA])
        m_ref[...] = m_new

    def body(j, c):
        step(j, None)
        return c

    lax.fori_loop(0, i, body, 0)
    step(i, s_i <= t_i)
    o_lat = (acc_ref[...] * (1.0 / l_ref[:, 0:1])).astype(BF16)
    for hd in range(C_HEADS):
        o_ref[:, hd * C_V_DIM:(hd + 1) * C_V_DIM] = _dot(o_lat[hd * QB:(hd + 1) * QB], wuv_ref[hd])


def _mla_prompt(qc, kc, wuv, nb, tp):
    nt = nb * tp
    nq = tp // QB
    return pl.pallas_call(
        _mla_prompt_kernel,
        grid=(nb, nq),
        in_specs=[pl.BlockSpec((QB, C_HEADS * C_QK), lambda b, i: (b * nq + i, 0)),
                  pl.BlockSpec((tp, C_QK), lambda b, i: (b, 0)),
                  pl.BlockSpec(wuv.shape, lambda b, i: (0, 0, 0))],
        out_specs=pl.BlockSpec((QB, C_WIDTH), lambda b, i: (b * nq + i, 0)),
        out_shape=jax.ShapeDtypeStruct((nt, C_WIDTH), F32),
        scratch_shapes=[pltpu.VMEM((C_HEADS * QB, C_QK), BF16), pltpu.VMEM((C_HEADS * QB, 1), F32),
                        pltpu.VMEM((C_HEADS * QB, LANES), F32), pltpu.VMEM((C_HEADS * QB, C_KV_LORA), F32)],
        compiler_params=_cparams(("parallel", "arbitrary"), 32),
        name="mla_prompt",
    )(qc, kc, wuv)


def _mla_sample_kernel(pt_ref, q_ref, cn_ref, pn_ref, *rest, n_grp):
    c_refs = rest[:n_grp]
    p_refs = rest[n_grp:2 * n_grp]
    wuv_ref, o_ref, m_ref, l_ref, acc_ref = rest[2 * n_grp:]
    s = pl.program_id(1)
    ts = q_ref.shape[0]
    rows = C_HEADS * ts
    q = q_ref[...]
    qh = jnp.concatenate([q[:, h * C_QK:(h + 1) * C_QK] for h in range(C_HEADS)], axis=0)
    q_lat = qh[:, 0:C_KV_LORA].astype(BF16)
    q_pe = qh[:, C_KV_LORA:C_KV_LORA + C_ROPE_DIM].astype(BF16)

    def update(sc, v):
        m_old = m_ref[...]
        m_new = jnp.maximum(m_old, jnp.max(sc, axis=1, keepdims=True))
        alpha = jnp.exp(m_old - m_new)
        p = jnp.exp(sc - m_new)
        l_ref[...] = alpha * l_ref[...] + jnp.sum(p, axis=1, keepdims=True)
        acc_ref[...] = alpha * acc_ref[...] + _dot(p.astype(BF16), v)
        m_ref[...] = m_new

    @pl.when(s == 0)
    def _():
        m_ref[...] = jnp.full_like(m_ref, NEG_INF)
        l_ref[...] = jnp.zeros_like(l_ref)
        acc_ref[...] = jnp.zeros_like(acc_ref)
        cn = jnp.concatenate([cn_ref[...], jnp.zeros((PAGE - ts, C_KV_LORA), F32)], axis=0).astype(BF16)
        pn = jnp.concatenate([pn_ref[...], jnp.zeros((PAGE - ts, C_ROPE_DIM), F32)], axis=0).astype(BF16)
        t_i = lax.broadcasted_iota(I32, (rows, PAGE), 0) % ts
        s_i = lax.broadcasted_iota(I32, (rows, PAGE), 1)
        update(jnp.where(s_i <= t_i, _dot_nt(q_lat, cn) + _dot_nt(q_pe, pn), NEG_INF), cn)

    ckv = jnp.concatenate([c_refs[g][...] for g in range(n_grp)], axis=0).astype(BF16)
    kpe_t = jnp.concatenate([p_refs[g][...] for g in range(n_grp)], axis=1).astype(BF16)
    update(_dot_nt(q_lat, ckv) + _dot(q_pe, kpe_t), ckv)

    @pl.when(s == pl.num_programs(1) - 1)
    def _():
        o_lat = (acc_ref[...] / l_ref[...]).astype(BF16)
        for hd in range(C_HEADS):
            o_ref[:, hd * C_V_DIM:(hd + 1) * C_V_DIM] = _dot(o_lat[hd * ts:(hd + 1) * ts], wuv_ref[hd])


def _mla_sample(pt_flat, qc, ckv, kpe, cache_ckv, cache_kpe, wuv, n_prompt_rows, dec_b, ts, n_pages, n_grp):
    row0 = n_prompt_rows // ts
    newrow = lambda b, s, pt: (row0 + b, 0)

    def page_spec(shape, g):
        return pl.BlockSpec((None,) + shape, lambda b, s, pt, g=g: (pt[b * n_pages + s * n_grp + g], 0, 0))

    grid_spec = pltpu.PrefetchScalarGridSpec(
        num_scalar_prefetch=1,
        grid=(dec_b, n_pages // n_grp),
        in_specs=[pl.BlockSpec((ts, C_HEADS * C_QK), newrow), pl.BlockSpec((ts, C_KV_LORA), newrow),
                  pl.BlockSpec((ts, C_ROPE_DIM), newrow)]
                 + [page_spec((PAGE, C_KV_LORA), g) for g in range(n_grp)]
                 + [page_spec((C_ROPE_DIM, PAGE), g) for g in range(n_grp)]
                 + [pl.BlockSpec(wuv.shape, lambda b, s, pt: (0, 0, 0))],
        out_specs=pl.BlockSpec((ts, C_WIDTH), lambda b, s, pt: (b, 0)),
        scratch_shapes=[pltpu.VMEM((C_HEADS * ts, 1), F32), pltpu.VMEM((C_HEADS * ts, 1), F32),
                        pltpu.VMEM((C_HEADS * ts, C_KV_LORA), F32)],
    )
    return pl.pallas_call(
        functools.partial(_mla_sample_kernel, n_grp=n_grp),
        grid_spec=grid_spec,
        out_shape=jax.ShapeDtypeStruct((dec_b * ts, C_WIDTH), F32),
        compiler_params=_cparams(("parallel", "arbitrary"), 32),
        name="mla_sample",
    )(pt_flat, qc, ckv, kpe, *([cache_ckv] * n_grp), *([cache_kpe] * n_grp), wuv)


def _ln_silu(y, g, b):
    yc = y - jnp.mean(y, axis=-1, keepdims=True)
    yn = yc * lax.rsqrt(jnp.mean(yc * yc, axis=-1, keepdims=True) + LN_EPS) * g + b
    return yn * jax.nn.sigmoid(yn)


def _conv_prompt_kernel(u_ref, halo_ref, cw_ref, cb_ref, g_ref, b_ref, o_ref, ext_ref, y_ref):
    i = pl.program_id(1)
    tb = u_ref.shape[0]
    ext_ref[0:CONV_HALO, :] = jnp.where(i > 0, halo_ref[...], 0.0)
    ext_ref[CONV_HALO:, :] = u_ref[...]
    off = CONV_HALO - (CONV_W - 1)
    cw = 256
    for c0 in range(0, D_CONV, cw):
        acc = jnp.zeros((tb, cw), F32) + cb_ref[:, c0:c0 + cw]
        for k in range(CONV_W):
            acc = acc + ext_ref[off + k:off + k + tb, c0:c0 + cw] * cw_ref[k:k + 1, c0:c0 + cw]
        y_ref[:, c0:c0 + cw] = acc
    o_ref[...] = _ln_silu(y_ref[...], g_ref[...], b_ref[...])


def _conv_prompt(u, cw, cb, g, b, nb, tp):
    nt = nb * tp
    tb = QB
    nq = tp // tb
    hb = tb // CONV_HALO
    const = lambda bb, i: (0, 0)
    return pl.pallas_call(
        _conv_prompt_kernel,
        grid=(nb, nq),
        in_specs=[pl.BlockSpec((tb, D_CONV), lambda bb, i: (bb * nq + i, 0)),
                  pl.BlockSpec((CONV_HALO, D_CONV), lambda bb, i: (jnp.maximum((bb * nq + i) * hb - 1, 0), 0)),
                  pl.BlockSpec(cw.shape, const), pl.BlockSpec((1, D_CONV), const), pl.BlockSpec((1, D_CONV), const),
                  pl.BlockSpec((1, D_CONV), const)],
        out_specs=pl.BlockSpec((tb, D_CONV), lambda bb, i: (bb * nq + i, 0)),
        out_shape=jax.ShapeDtypeStruct((nt, D_CONV), F32),
        scratch_shapes=[pltpu.VMEM((CONV_HALO + tb, D_CONV), F32), pltpu.VMEM((tb, D_CONV), F32)],
        compiler_params=_cparams(("parallel", "parallel"), 32),
        name="conv_prompt",
    )(u, u, cw, cb, g, b)


def _conv_sample_kernel(ext_ref, cw_ref, cb_ref, g_ref, b_ref, o_ref):
    bb, n_ext, _ = ext_ref.shape
    ts = n_ext - (CONV_W - 1)
    acc = jnp.zeros((bb, ts, D_CONV), F32) + cb_ref[...]
    for k in range(CONV_W):
        acc = acc + ext_ref[:, k:k + ts, :] * cw_ref[k:k + 1, :]
    o_ref[...] = _ln_silu(acc.reshape(bb * ts, D_CONV), g_ref[...], b_ref[...])


def _conv_sample(ext, cw, cb, g, b, bb):
    dec_b, n_ext, _ = ext.shape
    ts = n_ext - (CONV_W - 1)
    const = lambda i: (0, 0)
    return pl.pallas_call(
        _conv_sample_kernel,
        grid=(dec_b // bb,),
        in_specs=[pl.BlockSpec((bb, n_ext, D_CONV), lambda i: (i, 0, 0)),
                  pl.BlockSpec(cw.shape, const), pl.BlockSpec((1, D_CONV), const), pl.BlockSpec((1, D_CONV), const),
                  pl.BlockSpec((1, D_CONV), const)],
        out_specs=pl.BlockSpec((bb * ts, D_CONV), lambda i: (i, 0)),
        out_shape=jax.ShapeDtypeStruct((dec_b * ts, D_CONV), F32),
        compiler_params=_cparams(("parallel",), 32),
        name="conv_sample",
    )(ext, cw, cb, g, b)


def _router_weights(router_group, router_group_bias, router_expert, router_expert_bias):
    pad = LANES - N_EXPERTS - N_GROUPS
    w = jnp.concatenate([router_expert, router_group, jnp.zeros((D_MODEL, pad), F32)], axis=1)
    b = jnp.concatenate([router_expert_bias, router_group_bias, jnp.zeros((pad,), F32)]).reshape(1, LANES)
    wh = w.astype(BF16)
    wl = (w - wh.astype(F32)).astype(BF16)
    return wh, wl, b


def _cumsum_matrix():
    j = jnp.arange(2 * QB)[:, None] % QB
    c = jnp.arange(2 * QB)[None, :]
    return jnp.where((c >= QB) | (j > c), 1.0, 0.0).astype(BF16)


def kernel(x_prompt, x_sample, cache_sb_k, cache_sb_v, state_pool, cache_mla_ckv, cache_mla_kpe, state_conv, page_table, meta_tokens, norm_mix, norm_ffn, norm_final, w_in_ab, w_pool, pool_scale, w_out_ab, w_in_cd, mla_q_norm, mla_kv_norm, w_uq, w_uk, w_uv, conv_w, conv_b, conv_norm_g, conv_norm_b, w_out_cd, router_group, router_group_bias, router_expert, router_expert_bias, expert_w_gate, expert_w_up, expert_w_down):
    nb, seq, _ = x_prompt.shape
    dec_b, ts, _ = x_sample.shape
    n_pages = page_table.shape[1]
    past_len = n_pages * PAGE
    t_real = seq + N_META
    tp = -(-t_real // QB) * QB
    n_p = nb * tp
    n_s = dec_b * ts
    nt = n_p + n_s
    tm = 256 if nt % 256 == 0 else 128
    assert ts == 8 and n_p % tm == 0 and n_s % tm == 0 and past_len >= POOL_MAX

    meta = jnp.broadcast_to(meta_tokens[None], (nb, N_META, D_MODEL))
    xp = jnp.concatenate([meta, x_prompt, jnp.zeros((nb, tp - t_real, D_MODEL), F32)], axis=1)
    x = jnp.concatenate([xp.reshape(n_p, D_MODEL), x_sample.reshape(n_s, D_MODEL)], axis=0)
    pt_flat = page_table.reshape(-1).astype(I32)
    uo = _cumsum_matrix()
    grp = 8 if n_pages % 8 == 0 else 1
    bb = 16 if dec_b % 16 == 0 else dec_b

    def prompt_rows(a, n_last):
        return a[:n_p].reshape(nb, tp, -1)[:, t_real - n_last:t_real]

    def moe_layer(layer, x_res, a_p, a_s, b_p, b_s, w_out):
        rwh, rwl, rb = _router_weights(router_group[layer], router_group_bias[layer], router_expert[layer],
                                       router_expert_bias[layer])
        x_res, hn, ri, rf, cnt = _out_proj(x_res, a_p, a_s, b_p, b_s, w_out.astype(BF16),
                                           norm_ffn[layer].reshape(1, -1), rwh, rwl, rb, tm)
        sched = _moe_schedule(ri, cnt, nt)
        o = _moe(hn, *sched, expert_w_gate, expert_w_up, expert_w_down, layer)
        return x_res, o, rf

    q, k, v, u = _in_proj_ab(x, norm_mix[0].reshape(1, -1), w_in_ab.astype(BF16), tm)
    oa_p = _sb_prompt(q, k, v, uo, nb, tp)
    page_rows = (-1, PAGE * A_KV_HEADS, A_HEAD_DIM)
    oa_s = _sb_sample(pt_flat, q, k, v, cache_sb_k.reshape(page_rows), cache_sb_v.reshape(page_rows), uo, n_p, dec_b, ts,
                      n_pages, grp)
    wp = w_pool.astype(BF16)
    ps = pool_scale.reshape(1, -1)
    ob_p = _pool_prompt(u, wp, ps, nb, tp)
    pool_ext = jnp.concatenate([state_pool, u[n_p:].reshape(dec_b, ts, B_WIDTH)], axis=1)
    ob_s = _pool_sample(pool_ext, wp, ps, bb)
    x, o, rf = moe_layer(0, x, oa_p, oa_s, ob_p, ob_s, w_out_ab)

    o1 = C_Q_LORA
    o2 = o1 + C_KV_LORA
    o3 = o2 + C_ROPE_DIM
    w_cd = jnp.concatenate([w_in_cd[:, :o2], w_in_cd[:, o3:], w_in_cd[:, o2:o3],
                            jnp.zeros((D_MODEL, LANES - C_ROPE_DIM), F32)], axis=1).astype(BF16)
    wq = w_uq.reshape(C_Q_LORA, C_HEADS, C_NOPE_DIM + C_ROPE_DIM)
    wq_pe = jnp.concatenate([wq[:, :, C_NOPE_DIM:], jnp.zeros((C_Q_LORA, C_HEADS, LANES - C_ROPE_DIM), F32)], axis=2)
    wuq = jnp.concatenate([wq[:, :, :C_NOPE_DIM].reshape(C_Q_LORA, -1), wq_pe.reshape(C_Q_LORA, -1)], axis=1).astype(BF16)
    wuk = jnp.transpose(w_uk, (1, 2, 0)).astype(BF16)
    wuv = jnp.transpose(w_uv, (1, 0, 2)).astype(BF16)
    half = C_ROPE_DIM // 2
    inv_freq = ROPE_THETA ** (-jnp.arange(half, dtype=F32) / half)
    pos = jnp.concatenate([jnp.tile(jnp.arange(tp), nb), jnp.tile(past_len + jnp.arange(ts), dec_b)]).astype(F32)
    ang = pos[:, None] * inv_freq[None, :]
    zpad = jnp.zeros((nt, LANES - C_ROPE_DIM), F32)
    cos = jnp.concatenate([jnp.cos(ang), jnp.cos(ang), zpad], axis=1)
    sin = jnp.concatenate([jnp.sin(ang), jnp.sin(ang), zpad], axis=1)
    x, qc, ckv, kpe, kc, uc = _in_proj_cd(x, o, o, rf, norm_mix[1].reshape(1, -1), w_cd, mla_q_norm.reshape(1, -1),
                                          mla_kv_norm.reshape(1, -1), wuq, wuk, cos, sin, tm)
    oc_p = _mla_prompt(qc, kc, wuv, nb, tp)
    grp_c = 16 if n_pages % 16 == 0 else grp
    oc_s = _mla_sample(pt_flat, qc, ckv, kpe, cache_mla_ckv, jnp.swapaxes(cache_mla_kpe, 1, 2), wuv, n_p, dec_b, ts,
                       n_pages, grp_c)
    conv_args = (conv_w, conv_b.reshape(1, -1), conv_norm_g.reshape(1, -1), conv_norm_b.reshape(1, -1))
    od_p = _conv_prompt(uc, *conv_args, nb, tp)
    conv_ext = jnp.concatenate([state_conv, uc[n_p:].reshape(dec_b, ts, D_CONV)], axis=1)
    od_s = _conv_sample(conv_ext, *conv_args, bb)
    x, o, rf = moe_layer(1, x, oc_p, oc_s, od_p, od_s, w_out_cd)

    y = _final_norm(x, o, rf, norm_final.reshape(1, -1), tm)

    y_prompt = y[:n_p].reshape(nb, tp, D_MODEL)[:, N_META:t_real]
    y_sample = y[n_p:].reshape(dec_b, ts, D_MODEL)
    kv4 = (A_KV_HEADS, A_HEAD_DIM)
    return (y_prompt, y_sample,
            prompt_rows(k, t_real).reshape(nb, t_real, *kv4), prompt_rows(v, t_real).reshape(nb, t_real, *kv4),
            k[n_p:].reshape(dec_b, ts, *kv4), v[n_p:].reshape(dec_b, ts, *kv4),
            prompt_rows(u, POOL_MAX - 1), pool_ext[:, ts:],
            prompt_rows(ckv, t_real), prompt_rows(kpe, t_real),
            ckv[n_p:].reshape(dec_b, ts, C_KV_LORA), kpe[n_p:].reshape(dec_b, ts, C_ROPE_DIM),
            prompt_rows(uc, CONV_W - 1), conv_ext[:, ts:])
```

```python
import functools

import jax
import jax.numpy as jnp
from jax import lax
from jax.experimental import pallas as pl
from jax.experimental.pallas import tpu as pltpu

F32 = jnp.float32
BF16 = jnp.bfloat16
I32 = jnp.int32

D_MODEL = 2048
N_META = 16
RMS_EPS = 1e-6
LN_EPS = 1e-5
NEG_INF = -1e30
PAGE = 128
LANES = 128
QB = 128

A_HEADS = 8
A_KV_HEADS = 4
A_HEAD_DIM = 128
A_WIDTH = A_HEADS * A_HEAD_DIM
A_KV_WIDTH = A_KV_HEADS * A_HEAD_DIM
A_SCALE = A_HEAD_DIM ** -0.5

POOL_WINDOWS = (2, 4, 8, 16)
POOL_MAX = 16
B_WIDTH = 1024
B_GROUP_WIDTH = B_WIDTH // len(POOL_WINDOWS)

C_HEADS = 8
C_NOPE_DIM = 128
C_ROPE_DIM = 64
C_V_DIM = 128
C_Q_LORA = 512
C_KV_LORA = 256
C_WIDTH = C_HEADS * C_V_DIM
C_SCALE = (C_NOPE_DIM + C_ROPE_DIM) ** -0.5
ROPE_THETA = 10000.0
C_QK = C_KV_LORA + LANES

D_CONV = 1024
CONV_W = 31
CONV_HALO = 32

N_GROUPS = 4
EXPERTS_PER_GROUP = 8
N_EXPERTS = 32
D_EXPERT = 512
SB_PAGES_PER_STEP = 16
MLA_PAGES_PER_STEP = 32
MOE_TILE = 256
MOE_DMA_UNROLL = 8


def _cparams(sem, vmem_mb):
    return pltpu.CompilerParams(dimension_semantics=sem, vmem_limit_bytes=vmem_mb << 20)


def _rms(x, g):
    return x * lax.rsqrt(jnp.mean(x * x, axis=-1, keepdims=True) + RMS_EPS) * g


def _dot(a, b):
    return jnp.dot(a, b, preferred_element_type=F32)


def _dot_nt(a, b):
    return lax.dot_general(a, b, (((1,), (1,)), ((), ())), preferred_element_type=F32)


def _softplus(z):
    return jnp.maximum(z, 0.0) + jnp.log(1.0 + jnp.exp(-jnp.abs(z)))


def _moe_combine(x_ref, o1_ref, o2_ref, rf_ref):
    rf = rf_ref[...]
    return x_ref[...] + rf[:, 0:1] * o1_ref[...] + rf[:, 1:2] * o2_ref[...]


def _in_ab_kernel(x_ref, g_ref, w_ref, q_ref, k_ref, v_ref, u_ref):
    h = _rms(x_ref[...], g_ref[...]).astype(BF16)
    q_ref[...] = _dot(h, w_ref[:, 0:A_WIDTH]) * A_SCALE
    k_ref[...] = _dot(h, w_ref[:, A_WIDTH:A_WIDTH + A_KV_WIDTH])
    v_ref[...] = _dot(h, w_ref[:, A_WIDTH + A_KV_WIDTH:A_WIDTH + 2 * A_KV_WIDTH])
    u_ref[...] = _dot(h, w_ref[:, A_WIDTH + 2 * A_KV_WIDTH:])


def _in_proj_ab(x, g, w, tm):
    nt = x.shape[0]
    n_in = w.shape[1]
    row = lambda i: (i, 0)
    const = lambda i: (0, 0)
    return pl.pallas_call(
        _in_ab_kernel,
        grid=(nt // tm,),
        in_specs=[pl.BlockSpec((tm, D_MODEL), row), pl.BlockSpec((1, D_MODEL), const),
                  pl.BlockSpec((D_MODEL, n_in), const)],
        out_specs=[pl.BlockSpec((tm, A_WIDTH), row), pl.BlockSpec((tm, A_KV_WIDTH), row),
                   pl.BlockSpec((tm, A_KV_WIDTH), row), pl.BlockSpec((tm, B_WIDTH), row)],
        out_shape=[jax.ShapeDtypeStruct((nt, A_WIDTH), F32), jax.ShapeDtypeStruct((nt, A_KV_WIDTH), F32),
                   jax.ShapeDtypeStruct((nt, A_KV_WIDTH), F32), jax.ShapeDtypeStruct((nt, B_WIDTH), F32)],
        compiler_params=_cparams(("parallel",), 48),
        name="in_proj_ab",
    )(x, g, w)


def _rot_half(x, lane_period):
    n = x.shape[-1]
    half = C_ROPE_DIM // 2
    lane = lax.broadcasted_iota(I32, x.shape, x.ndim - 1) % lane_period
    fwd = pltpu.roll(x, n - half, x.ndim - 1)
    bwd = pltpu.roll(x, half, x.ndim - 1)
    return jnp.where(lane < half, -fwd, jnp.where(lane < C_ROPE_DIM, bwd, 0.0))


def _in_cd_kernel(x_ref, o1_ref, o2_ref, rf_ref, g_ref, w_ref, qn_ref, kvn_ref, wuq_ref, wuk_ref, cos_ref, sin_ref,
                  xo_ref, qc_ref, ckv_ref, kpe_ref, kc_ref, u_ref):
    x = _moe_combine(x_ref, o1_ref, o2_ref, rf_ref)
    xo_ref[...] = x
    h = _rms(x, g_ref[...]).astype(BF16)
    cos = cos_ref[...]
    sin = sin_ref[...]
    cq = _rms(_dot(h, w_ref[:, 0:C_Q_LORA]), qn_ref[...]).astype(BF16)
    for hd in range(C_HEADS):
        qn = _dot(cq, wuq_ref[:, hd * LANES:(hd + 1) * LANES]).astype(BF16)
        qc_ref[:, hd * C_QK:hd * C_QK + C_KV_LORA] = _dot(qn, wuk_ref[hd]) * C_SCALE
        qp = _dot(cq, wuq_ref[:, (C_HEADS + hd) * LANES:(C_HEADS + hd + 1) * LANES])
        qc_ref[:, hd * C_QK + C_KV_LORA:(hd + 1) * C_QK] = (qp * cos + _rot_half(qp, LANES) * sin) * C_SCALE
    o1 = C_Q_LORA
    o2 = o1 + C_KV_LORA
    ckv = _rms(_dot(h, w_ref[:, o1:o2]), kvn_ref[...])
    ckv_ref[...] = ckv
    o3 = o2 + 2 * D_CONV
    kr = _dot(h, w_ref[:, o3:o3 + LANES])
    kpe = kr * cos + _rot_half(kr, LANES) * sin
    kpe_ref[...] = kpe[:, 0:C_ROPE_DIM]
    kc_ref[:, 0:C_KV_LORA] = ckv.astype(BF16)
    kc_ref[:, C_KV_LORA:] = kpe.astype(BF16)
    ga = _dot(h, w_ref[:, o2:o2 + D_CONV])
    gb = _dot(h, w_ref[:, o2 + D_CONV:o3])
    u_ref[...] = ga * jax.nn.sigmoid(gb)


def _in_proj_cd(x, o1, o2, rf, g, w, qn, kvn, wuq, wuk, cos, sin, tm):
    nt = x.shape[0]
    nb = nt // tm
    row = lambda i: (i, 0)
    row2 = lambda i: (i + nb, 0)
    const = lambda i: (0, 0)
    return pl.pallas_call(
        _in_cd_kernel,
        grid=(nb,),
        in_specs=[pl.BlockSpec((tm, D_MODEL), row), pl.BlockSpec((tm, D_MODEL), row), pl.BlockSpec((tm, D_MODEL), row2),
                  pl.BlockSpec((tm, LANES), row), pl.BlockSpec((1, D_MODEL), const),
                  pl.BlockSpec(w.shape, const), pl.BlockSpec((1, C_Q_LORA), const), pl.BlockSpec((1, C_KV_LORA), const),
                  pl.BlockSpec(wuq.shape, const), pl.BlockSpec(wuk.shape, lambda i: (0, 0, 0)),
                  pl.BlockSpec((tm, LANES), row), pl.BlockSpec((tm, LANES), row)],
        out_specs=[pl.BlockSpec((tm, D_MODEL), row), pl.BlockSpec((tm, C_HEADS * C_QK), row),
                   pl.BlockSpec((tm, C_KV_LORA), row), pl.BlockSpec((tm, C_ROPE_DIM), row),
                   pl.BlockSpec((tm, C_QK), row), pl.BlockSpec((tm, D_CONV), row)],
        out_shape=[jax.ShapeDtypeStruct((nt, D_MODEL), F32), jax.ShapeDtypeStruct((nt, C_HEADS * C_QK), F32),
                   jax.ShapeDtypeStruct((nt, C_KV_LORA), F32), jax.ShapeDtypeStruct((nt, C_ROPE_DIM), F32),
                   jax.ShapeDtypeStruct((nt, C_QK), BF16), jax.ShapeDtypeStruct((nt, D_CONV), F32)],
        compiler_params=_cparams(("parallel",), 56),
        name="in_proj_cd",
    )(x, o1, o2, rf, g, w, qn, kvn, wuq, wuk, cos, sin)


def _out_proj_kernel(x_ref, ap_ref, as_ref, bp_ref, bs_ref, w_ref, g_ref, rwh_ref, rwl_ref, rb_ref,
                     xo_ref, hn_ref, ri_ref, rf_ref, cnt_ref, carry_ref, *, n_prompt_tiles):
    tm = x_ref.shape[0]
    step = pl.program_id(0)

    @pl.when(step == 0)
    def _():
        carry_ref[...] = jnp.zeros_like(carry_ref)

    is_prompt = step < n_prompt_tiles
    a = jnp.where(is_prompt, ap_ref[...], as_ref[...])
    b = jnp.where(is_prompt, bp_ref[...], bs_ref[...])
    ab = jnp.concatenate([a, b], axis=1).astype(BF16)
    x = x_ref[...] + _dot(ab, w_ref[...])
    xo_ref[...] = x
    hn = _rms(x, g_ref[...])
    hn_ref[...] = hn
    hh = hn.astype(BF16)
    hl = (hn - hh.astype(F32)).astype(BF16)
    logit = _dot(hh, rwh_ref[...]) + _dot(hl, rwh_ref[...]) + _dot(hh, rwl_ref[...]) + rb_ref[...]
    lane = lax.broadcasted_iota(I32, (tm, LANES), 1)
    lanef = lane.astype(F32)
    big = 1e9
    is_g = (lane >= N_EXPERTS) & (lane < N_EXPERTS + N_GROUPS)
    gl = jnp.where(is_g, logit, NEG_INF)
    gmax = jnp.max(gl, axis=1, keepdims=True)
    grp_lane = jnp.min(jnp.where(gl == gmax, lanef, big), axis=1, keepdims=True)
    g_w = 1.0 / jnp.sum(jnp.where(is_g, jnp.exp(gl - gmax), 0.0), axis=1, keepdims=True)
    grp = grp_lane.astype(I32) - N_EXPERTS
    in_grp = (lane < N_EXPERTS) & (jnp.right_shift(lane, 3) == grp)
    el = jnp.where(in_grp, logit, NEG_INF)
    v1 = jnp.max(el, axis=1, keepdims=True)
    i1 = jnp.min(jnp.where(el == v1, lanef, big), axis=1, keepdims=True)
    el2 = jnp.where(lanef == i1, NEG_INF, el)
    v2 = jnp.max(el2, axis=1, keepdims=True)
    i2 = jnp.min(jnp.where(el2 == v2, lanef, big), axis=1, keepdims=True)
    e21 = jnp.exp(v2 - v1)
    gate1 = g_w / (1.0 + e21)
    gate2 = gate1 * e21
    oh1 = lanef == i1
    oh2 = lanef == i2
    cf = jnp.where(oh1 | oh2, 1.0, 0.0)
    r_i = lax.broadcasted_iota(I32, (tm, tm), 0)
    c_i = lax.broadcasted_iota(I32, (tm, tm), 1)
    tri = jnp.where(r_i > c_i, 1.0, 0.0).astype(BF16)
    carry = carry_ref[...]
    cum = _dot(tri, cf.astype(BF16)) + carry[0:1, :]
    rank1 = jnp.sum(jnp.where(oh1, cum, 0.0), axis=1, keepdims=True)
    rank2 = jnp.sum(jnp.where(oh2, cum, 0.0), axis=1, keepdims=True)
    carry = carry + jnp.sum(cf, axis=0, keepdims=True)
    carry_ref[...] = carry
    cnt_ref[...] = carry
    ri = jnp.where(lane == 0, i1, jnp.where(lane == 1, i2, jnp.where(lane == 2, rank1, jnp.where(lane == 3, rank2, 0.0))))
    ri_ref[...] = ri.astype(I32)
    rf_ref[...] = jnp.where(lane == 0, gate1, jnp.where(lane == 1, gate2, 0.0))


def _out_proj(x, a_p, a_s, b_p, b_s, w, g, rwh, rwl, rb, tm):
    nt = x.shape[0]
    nbp = a_p.shape[0] // tm
    row = lambda i: (i, 0)
    prow = lambda i: (jnp.minimum(i, nbp - 1), 0)
    srow = lambda i: (jnp.maximum(i - nbp, 0), 0)
    const = lambda i: (0, 0)
    return pl.pallas_call(
        functools.partial(_out_proj_kernel, n_prompt_tiles=nbp),
        grid=(nt // tm,),
        in_specs=[pl.BlockSpec((tm, D_MODEL), row), pl.BlockSpec((tm, a_p.shape[1]), prow),
                  pl.BlockSpec((tm, a_s.shape[1]), srow), pl.BlockSpec((tm, b_p.shape[1]), prow),
                  pl.BlockSpec((tm, b_s.shape[1]), srow),
                  pl.BlockSpec(w.shape, const), pl.BlockSpec((1, D_MODEL), const),
                  pl.BlockSpec((D_MODEL, LANES), const), pl.BlockSpec((D_MODEL, LANES), const), pl.BlockSpec((1, LANES), const)],
        out_specs=[pl.BlockSpec((tm, D_MODEL), row), pl.BlockSpec((tm, D_MODEL), row),
                   pl.BlockSpec((tm, LANES), row), pl.BlockSpec((tm, LANES), row), pl.BlockSpec((8, LANES), const)],
        out_shape=[jax.ShapeDtypeStruct((nt, D_MODEL), F32), jax.ShapeDtypeStruct((nt, D_MODEL), F32),
                   jax.ShapeDtypeStruct((nt, LANES), I32), jax.ShapeDtypeStruct((nt, LANES), F32),
                   jax.ShapeDtypeStruct((8, LANES), F32)],
        scratch_shapes=[pltpu.VMEM((8, LANES), F32)],
        compiler_params=_cparams(("arbitrary",), 48),
        name="out_proj_router",
    )(x, a_p, a_s, b_p, b_s, w, g, rwh, rwl, rb)


def _final_norm_kernel(x_ref, o1_ref, o2_ref, rf_ref, g_ref, y_ref):
    y_ref[...] = _rms(_moe_combine(x_ref, o1_ref, o2_ref, rf_ref), g_ref[...])


def _final_norm(x, o, rf, g, tm):
    nt = x.shape[0]
    nb = nt // tm
    row = lambda i: (i, 0)
    return pl.pallas_call(
        _final_norm_kernel,
        grid=(nb,),
        in_specs=[pl.BlockSpec((tm, D_MODEL), row), pl.BlockSpec((tm, D_MODEL), row),
                  pl.BlockSpec((tm, D_MODEL), lambda i: (i + nb, 0)), pl.BlockSpec((tm, LANES), row),
                  pl.BlockSpec((1, D_MODEL), lambda i: (0, 0))],
        out_specs=pl.BlockSpec((tm, D_MODEL), row),
        out_shape=jax.ShapeDtypeStruct((nt, D_MODEL), F32),
        compiler_params=_cparams(("parallel",), 32),
        name="final_norm",
    )(x, o, o, rf, g)


def _moe_kernel(te_ref, nu_ref, tv_ref, did_ref, hn_ref, wg_ref, wu_ref, wd_ref, o_ref,
                xbuf, ybuf, gsem, ssem):
    i = pl.program_id(0)
    n_used = nu_ref[0]
    tmm = ybuf.shape[0]
    n_tok = hn_ref.shape[0]

    def gather(tile, slot):
        base = tile * tmm

        def body(c8, c):
            for u in range(MOE_DMA_UNROLL):
                r = c8 * MOE_DMA_UNROLL + u
                did = did_ref[base + r]
                rid = jnp.where(did >= n_tok, did - n_tok, did)
                pltpu.make_async_copy(hn_ref.at[pl.ds(rid, 1), :], xbuf.at[slot, pl.ds(r, 1), :], gsem.at[slot]).start()
            return c

        lax.fori_loop(0, tmm // MOE_DMA_UNROLL, body, 0)

    def gather_wait(slot):
        pltpu.make_async_copy(hn_ref.at[pl.ds(0, tmm), :], xbuf.at[slot], gsem.at[slot]).wait()

    def scatter_wait(tile):
        n_valid = tv_ref[tile]

        @pl.when(n_valid == tmm)
        def _():
            pltpu.make_async_copy(ybuf, o_ref.at[pl.ds(0, tmm), :], ssem.at[0]).wait()

        @pl.when(n_valid < tmm)
        def _():
            def body(r, c):
                pltpu.make_async_copy(ybuf.at[pl.ds(0, 1), :], o_ref.at[pl.ds(0, 1), :], ssem.at[0]).wait()
                return c

            lax.fori_loop(0, n_valid, body, 0)

    @pl.when(i == 0)
    def _():
        gather(0, 0)

    @pl.when(i < n_used)
    def _():
        slot = i % 2

        @pl.when(i + 1 < n_used)
        def _():
            gather(i + 1, 1 - slot)

        gather_wait(slot)
        x = xbuf[slot].astype(BF16)
        hid = jax.nn.silu(_dot(x, wg_ref[...].astype(BF16))) * _dot(x, wu_ref[...].astype(BF16))
        y = _dot(hid.astype(BF16), wd_ref[...].astype(BF16))

        @pl.when(i > 0)
        def _():
            scatter_wait(i - 1)

        ybuf[...] = y
        base = i * tmm

        def scatter_row(r):
            did = did_ref[base + r]
            pltpu.make_async_copy(ybuf.at[pl.ds(r, 1), :], o_ref.at[pl.ds(did, 1), :], ssem.at[0]).start()

        def body_unrolled(c8, c):
            for u in range(MOE_DMA_UNROLL):
                scatter_row(c8 * MOE_DMA_UNROLL + u)
            return c

        def body_single(r, c):
            scatter_row(r)
            return c

        n_valid = tv_ref[i]
        n_chunks = n_valid // MOE_DMA_UNROLL
        lax.fori_loop(0, n_chunks, body_unrolled, 0)
        lax.fori_loop(n_chunks * MOE_DMA_UNROLL, n_valid, body_single, 0)

        @pl.when(i == n_used - 1)
        def _():
            scatter_wait(i)


def _moe(hn, tile_expert, n_used, tile_valid, dst_ids, w_gate, w_up, w_down, layer):
    nt = hn.shape[0]
    n_tiles = tile_expert.shape[0]
    wspec = lambda shp: pl.BlockSpec((None, None) + shp, lambda i, te, nu, tv, did: (layer, te[i], 0, 0))
    grid_spec = pltpu.PrefetchScalarGridSpec(
        num_scalar_prefetch=4,
        grid=(n_tiles,),
        in_specs=[pl.BlockSpec(memory_space=pl.ANY), wspec((D_MODEL, D_EXPERT)), wspec((D_MODEL, D_EXPERT)),
                  wspec((D_EXPERT, D_MODEL))],
        out_specs=pl.BlockSpec(memory_space=pl.ANY),
        scratch_shapes=[pltpu.VMEM((2, MOE_TILE, D_MODEL), F32), pltpu.VMEM((MOE_TILE, D_MODEL), F32),
                        pltpu.SemaphoreType.DMA((2,)), pltpu.SemaphoreType.DMA((1,))],
    )
    return pl.pallas_call(
        _moe_kernel,
        grid_spec=grid_spec,
        out_shape=jax.ShapeDtypeStruct((2 * nt, D_MODEL), F32),
        compiler_params=_cparams(("arbitrary",), 56),
        name="moe_experts",
    )(tile_expert, n_used, tile_valid, dst_ids, hn, w_gate, w_up, w_down)


def _moe_pos_kernel(ri_ref, offs_ref, pos_ref):
    ri = ri_ref[...]
    lane = lax.broadcasted_iota(I32, ri.shape, 1)
    offs = offs_ref[...]
    pos1 = jnp.sum(jnp.where(lane == ri[:, 0:1], offs, 0.0), axis=1, keepdims=True).astype(I32) + ri[:, 2:3]
    pos2 = jnp.sum(jnp.where(lane == ri[:, 1:2], offs, 0.0), axis=1, keepdims=True).astype(I32) + ri[:, 3:4]
    pos_ref[...] = jnp.where(lane == 0, pos1, jnp.where(lane == 1, pos2, 0))


def _moe_positions(ri, offs, tm):
    nt = ri.shape[0]
    offs_row = jnp.zeros((1, LANES), F32).at[0, :N_EXPERTS].set(offs.astype(F32))
    return pl.pallas_call(
        _moe_pos_kernel,
        grid=(nt // tm,),
        in_specs=[pl.BlockSpec((tm, LANES), lambda i: (i, 0)), pl.BlockSpec((1, LANES), lambda i: (0, 0))],
        out_specs=pl.BlockSpec((tm, LANES), lambda i: (i, 0)),
        out_shape=jax.ShapeDtypeStruct((nt, LANES), I32),
        compiler_params=_cparams(("parallel",), 32),
        name="moe_positions",
    )(ri, offs_row)


def _moe_schedule(ri, cnt, nt, tm):
    counts = cnt[0, :N_EXPERTS].astype(I32)
    padded = ((counts + MOE_TILE - 1) // MOE_TILE) * MOE_TILE
    ends = jnp.cumsum(padded)
    offs = ends - padded
    n_tiles = -(-2 * nt // MOE_TILE) + N_EXPERTS
    n_used = (ends[-1] // MOE_TILE).astype(I32)
    tile_raw = jnp.arange(n_tiles, dtype=I32)
    tile = jnp.minimum(tile_raw, n_used - 1)
    tile_expert = jnp.minimum(jnp.sum((ends // MOE_TILE)[None, :] <= tile[:, None], axis=1), N_EXPERTS - 1).astype(I32)
    rows_left = counts[tile_expert] - (tile * MOE_TILE - offs[tile_expert])
    tile_valid = jnp.where(tile_raw < n_used, jnp.clip(rows_left, 0, MOE_TILE), 0).astype(I32)
    pos = _moe_positions(ri, offs, tm)
    n_sorted = n_tiles * MOE_TILE
    dst_ids = jnp.zeros((n_sorted,), I32).at[jnp.concatenate([pos[:, 0], pos[:, 1]])].set(
        jnp.arange(2 * nt, dtype=I32), unique_indices=True)
    return tile_expert, n_used.reshape(1), tile_valid, dst_ids


def _sb_blocks(q, kt, vt, uo, acc, mask):
    n_h = len(q)
    rows = q[0].shape[0]
    n_blk = kt[0].shape[0] // QB
    z = jnp.concatenate([_dot_nt(q[h], kt[h]) for h in range(n_h)], axis=0)
    sp = _softplus(z)
    lk = -sp
    if mask is not None:
        lk = jnp.where(mask, lk, 0.0)
    hi = lk.astype(BF16)
    lo = (lk - hi.astype(F32)).astype(BF16)
    split = jnp.concatenate([jnp.concatenate([hi[:, g * QB:(g + 1) * QB], lo[:, g * QB:(g + 1) * QB]], axis=1)
                             for g in range(n_blk)], axis=0)
    cs = _dot(split, uo)
    later = []
    for g in range(n_blk):
        cs_g = cs[g * n_h * rows:(g + 1) * n_h * rows]
        later.append(cs_g[:, 0:QB] + acc)
        acc = acc + cs_g[:, QB:]
    w = jnp.exp(z - sp + jnp.concatenate(later, axis=1))
    if mask is not None:
        w = jnp.where(mask, w, 0.0)
    w = w.astype(BF16)
    out = jnp.concatenate([_dot(w[h * rows:(h + 1) * rows], vt[h]) for h in range(n_h)], axis=0)
    return out, acc


def _sb_prompt_kernel(q_ref, k_ref, v_ref, uo_ref, o_ref, acc_ref, out_ref):
    i = pl.program_id(1)
    group = A_HEADS // A_KV_HEADS
    gw = group * A_HEAD_DIM
    q = [jnp.concatenate([q_ref[:, h * gw + g * A_HEAD_DIM:h * gw + (g + 1) * A_HEAD_DIM] for g in range(group)],
                         axis=0).astype(BF16) for h in range(A_KV_HEADS)]
    uo = uo_ref[...]
    t_i = lax.broadcasted_iota(I32, (A_HEADS * QB, QB), 0) % QB
    s_i = lax.broadcasted_iota(I32, (A_HEADS * QB, QB), 1)

    def step(j, mask):
        start = pl.multiple_of(j * QB, QB)
        kt = [k_ref[pl.ds(start, QB), h * A_HEAD_DIM:(h + 1) * A_HEAD_DIM].astype(BF16) for h in range(A_KV_HEADS)]
        vt = [v_ref[pl.ds(start, QB), h * A_HEAD_DIM:(h + 1) * A_HEAD_DIM].astype(BF16) for h in range(A_KV_HEADS)]
        return _sb_blocks(q, kt, vt, uo, acc_ref[...], mask)

    acc_ref[...] = jnp.zeros_like(acc_ref)
    out, acc = step(i, s_i < t_i)
    out_ref[...] = out
    acc_ref[...] = acc

    def body(jj, c):
        out, acc = step(i - 1 - jj, None)
        out_ref[...] += out
        acc_ref[...] = acc
        return c

    lax.fori_loop(0, i, body, 0)
    for hh in range(A_HEADS):
        o_ref[:, hh * A_HEAD_DIM:(hh + 1) * A_HEAD_DIM] = out_ref[hh * QB:(hh + 1) * QB, :]


def _sb_prompt(q, k, v, uo, nb, tp):
    nt = nb * tp
    nq = tp // QB
    return pl.pallas_call(
        _sb_prompt_kernel,
        grid=(nb, nq),
        in_specs=[pl.BlockSpec((QB, A_WIDTH), lambda b, i: (b * nq + i, 0)),
                  pl.BlockSpec((tp, A_KV_WIDTH), lambda b, i: (b, 0)),
                  pl.BlockSpec((tp, A_KV_WIDTH), lambda b, i: (b, 0)),
                  pl.BlockSpec((2 * QB, 2 * QB), lambda b, i: (0, 0))],
        out_specs=pl.BlockSpec((QB, A_WIDTH), lambda b, i: (b * nq + i, 0)),
        out_shape=jax.ShapeDtypeStruct((nt, A_WIDTH), F32),
        scratch_shapes=[pltpu.VMEM((A_HEADS * QB, QB), F32), pltpu.VMEM((A_HEADS * QB, QB), F32)],
        compiler_params=_cparams(("parallel", "arbitrary"), 40),
        name="sb_prompt",
    )(q, k, v, uo)


def _paged_prefetch(pt_ref, caches, bufs, sem, n_pages, n_grp, newest_first):
    n_steps = pl.num_programs(1)
    t = pl.program_id(0) * n_steps + pl.program_id(1)
    last = pl.num_programs(0) * n_steps - 1
    slot = t % 2

    def start(step, sl):
        seq = step // n_steps
        first = (step % n_steps) * n_grp
        for g in range(n_grp):
            idx = (n_pages - 1 - (first + g)) if newest_first else (first + g)
            page = pt_ref[seq * n_pages + idx]
            for k, (cache, buf) in enumerate(zip(caches, bufs)):
                pltpu.make_async_copy(cache.at[page], buf.at[sl, g], sem.at[k, sl]).start()

    def wait(sl):
        for k, (cache, buf) in enumerate(zip(caches, bufs)):
            pltpu.make_async_copy(cache.at[pl.ds(0, n_grp)], buf.at[sl], sem.at[k, sl]).wait()

    @pl.when(t == 0)
    def _():
        start(0, 0)

    start(jnp.minimum(t + 1, last), 1 - slot)
    wait(slot)
    return slot, t == last, lambda: wait(1 - slot)


def _sb_sample_kernel(pt_ref, q_ref, kn_ref, vn_ref, uo_ref, ck_ref, cv_ref, o_ref, kbuf, vbuf, sem, acc_ref, out_ref,
                      *, n_grp, n_pages):
    slot, is_last, drain = _paged_prefetch(pt_ref, (ck_ref, cv_ref), (kbuf, vbuf), sem, n_pages, n_grp, True)
    s = pl.program_id(1)
    ts = q_ref.shape[0]
    group = A_HEADS // A_KV_HEADS
    gw = group * A_HEAD_DIM
    qf = q_ref[...]
    q = [jnp.concatenate([qf[:, h * gw + g * A_HEAD_DIM:h * gw + (g + 1) * A_HEAD_DIM] for g in range(group)],
                         axis=0).astype(BF16) for h in range(A_KV_HEADS)]
    uo = uo_ref[...]
    n_rows = A_HEADS * ts

    @pl.when(s == 0)
    def _():
        pad = jnp.zeros((QB - ts, A_KV_WIDTH), F32)
        kn = jnp.concatenate([kn_ref[...], pad], axis=0)
        vn = jnp.concatenate([vn_ref[...], pad], axis=0)
        heads = lambda x: [x[:, h * A_HEAD_DIM:(h + 1) * A_HEAD_DIM].astype(BF16) for h in range(A_KV_HEADS)]
        t_i = lax.broadcasted_iota(I32, (n_rows, QB), 0) % ts
        s_i = lax.broadcasted_iota(I32, (n_rows, QB), 1)
        out, acc = _sb_blocks(q, heads(kn), heads(vn), uo, jnp.zeros((n_rows, QB), F32), s_i < t_i)
        out_ref[...] = out
        acc_ref[...] = acc

    def head_rows(buf, g, h):
        return buf[slot, g, pl.ds(h, PAGE, stride=A_KV_HEADS), :]

    kt = [jnp.concatenate([head_rows(kbuf, g, h) for g in range(n_grp)], axis=0).astype(BF16)
          for h in range(A_KV_HEADS)]
    vt = [jnp.concatenate([head_rows(vbuf, g, h) for g in range(n_grp)], axis=0).astype(BF16)
          for h in range(A_KV_HEADS)]
    out, acc = _sb_blocks(q, kt, vt, uo, acc_ref[...], None)
    out = out_ref[...] + out
    out_ref[...] = out
    acc_ref[...] = acc

    @pl.when(s == pl.num_programs(1) - 1)
    def _():
        for h in range(A_HEADS):
            o_ref[:, h * A_HEAD_DIM:(h + 1) * A_HEAD_DIM] = out[h * ts:(h + 1) * ts, :]

    @pl.when(is_last)
    def _():
        drain()


def _sb_sample(pt_flat, q, k, v, cache_k, cache_v, uo, n_prompt_rows, dec_b, ts, n_pages, n_grp):
    row0 = n_prompt_rows // ts
    newrow = lambda b, s, pt: (row0 + b, 0)

    page_shape = (PAGE * A_KV_HEADS, A_HEAD_DIM)
    grid_spec = pltpu.PrefetchScalarGridSpec(
        num_scalar_prefetch=1,
        grid=(dec_b, n_pages // n_grp),
        in_specs=[pl.BlockSpec((ts, A_WIDTH), newrow), pl.BlockSpec((ts, A_KV_WIDTH), newrow),
                  pl.BlockSpec((ts, A_KV_WIDTH), newrow), pl.BlockSpec((2 * QB, 2 * QB), lambda b, s, pt: (0, 0)),
                  pl.BlockSpec(memory_space=pl.ANY), pl.BlockSpec(memory_space=pl.ANY)],
        out_specs=pl.BlockSpec((ts, A_WIDTH), lambda b, s, pt: (b, 0)),
        scratch_shapes=[pltpu.VMEM((2, n_grp) + page_shape, F32), pltpu.VMEM((2, n_grp) + page_shape, F32),
                        pltpu.SemaphoreType.DMA((2, 2)),
                        pltpu.VMEM((A_HEADS * ts, QB), F32), pltpu.VMEM((A_HEADS * ts, QB), F32)],
    )
    return pl.pallas_call(
        functools.partial(_sb_sample_kernel, n_grp=n_grp, n_pages=n_pages),
        grid_spec=grid_spec,
        out_shape=jax.ShapeDtypeStruct((dec_b * ts, A_WIDTH), F32),
        compiler_params=_cparams(("arbitrary", "arbitrary"), 40),
        name="sb_sample",
    )(pt_flat, q, k, v, uo, cache_k, cache_v)


def _pool_project(d, gi, wp_ref, ps_ref):
    c0 = gi * B_GROUP_WIDTH
    return _dot(d.astype(BF16), wp_ref[gi]) * ps_ref[:, c0:c0 + B_GROUP_WIDTH]


def _pool_prompt_kernel(u_ref, halo_ref, wp_ref, ps_ref, o_ref, ext_ref):
    i = pl.program_id(1)
    tb = u_ref.shape[0]
    ext_ref[0:POOL_MAX, :] = jnp.where(i > 0, halo_ref[...], 0.0)
    ext_ref[POOL_MAX:, :] = u_ref[...]
    pos = (i * tb + lax.broadcasted_iota(I32, (tb, 1), 0)).astype(F32)
    for gi, win in enumerate(POOL_WINDOWS):
        c0 = gi * B_GROUP_WIDTH
        c1 = c0 + B_GROUP_WIDTH
        cur = ext_ref[POOL_MAX:, c0:c1]
        ws = cur
        for k in range(1, win):
            ws = ws + ext_ref[POOL_MAX - k:POOL_MAX - k + tb, c0:c1]
        d = ws * (1.0 / jnp.minimum(pos + 1.0, float(win))) - cur
        o_ref[:, c0:c1] = _pool_project(d, gi, wp_ref, ps_ref)


def _pool_prompt(u, wp, ps, nb, tp):
    nt = nb * tp
    tb = QB
    nq = tp // tb
    hb = tb // POOL_MAX
    return pl.pallas_call(
        _pool_prompt_kernel,
        grid=(nb, nq),
        in_specs=[pl.BlockSpec((tb, B_WIDTH), lambda b, i: (b * nq + i, 0)),
                  pl.BlockSpec((POOL_MAX, B_WIDTH), lambda b, i: (jnp.maximum((b * nq + i) * hb - 1, 0), 0)),
                  pl.BlockSpec(wp.shape, lambda b, i: (0, 0, 0)), pl.BlockSpec((1, B_WIDTH), lambda b, i: (0, 0))],
        out_specs=pl.BlockSpec((tb, B_WIDTH), lambda b, i: (b * nq + i, 0)),
        out_shape=jax.ShapeDtypeStruct((nt, B_WIDTH), F32),
        scratch_shapes=[pltpu.VMEM((POOL_MAX + tb, B_WIDTH), F32)],
        compiler_params=_cparams(("parallel", "parallel"), 32),
        name="pool_prompt",
    )(u, u, wp, ps)


def _pool_sample_kernel(ext_ref, wp_ref, ps_ref, o_ref):
    bb, n_ext, _ = ext_ref.shape
    ts = n_ext - (POOL_MAX - 1)
    for gi, win in enumerate(POOL_WINDOWS):
        c0 = gi * B_GROUP_WIDTH
        c1 = c0 + B_GROUP_WIDTH
        cur = ext_ref[:, POOL_MAX - 1:, c0:c1]
        ws = cur
        for k in range(1, win):
            ws = ws + ext_ref[:, POOL_MAX - 1 - k:POOL_MAX - 1 - k + ts, c0:c1]
        d = (ws * (1.0 / win) - cur).reshape(bb * ts, B_GROUP_WIDTH)
        o_ref[:, c0:c1] = _pool_project(d, gi, wp_ref, ps_ref)


def _pool_sample(ext, wp, ps, bb):
    dec_b, n_ext, _ = ext.shape
    ts = n_ext - (POOL_MAX - 1)
    return pl.pallas_call(
        _pool_sample_kernel,
        grid=(dec_b // bb,),
        in_specs=[pl.BlockSpec((bb, n_ext, B_WIDTH), lambda i: (i, 0, 0)),
                  pl.BlockSpec(wp.shape, lambda i: (0, 0, 0)), pl.BlockSpec((1, B_WIDTH), lambda i: (0, 0))],
        out_specs=pl.BlockSpec((bb * ts, B_WIDTH), lambda i: (i, 0)),
        out_shape=jax.ShapeDtypeStruct((dec_b * ts, B_WIDTH), F32),
        compiler_params=_cparams(("parallel",), 32),
        name="pool_sample",
    )(ext, wp, ps)


def _mla_prompt_kernel(q_ref, kc_ref, wuv_ref, o_ref, qs_ref, m_ref, l_ref, acc_ref):
    i = pl.program_id(1)
    n_rows = C_HEADS * QB
    t_i = lax.broadcasted_iota(I32, (n_rows, QB), 0) % QB
    s_i = lax.broadcasted_iota(I32, (n_rows, QB), 1)
    for hd in range(C_HEADS):
        qs_ref[hd * QB:(hd + 1) * QB, :] = q_ref[:, hd * C_QK:(hd + 1) * C_QK].astype(BF16)
    m_ref[...] = jnp.full_like(m_ref, NEG_INF)
    l_ref[...] = jnp.zeros_like(l_ref)
    acc_ref[...] = jnp.zeros_like(acc_ref)

    def step(j, n_blk, mask):
        start = pl.multiple_of(j * QB, QB)
        kc = kc_ref[pl.ds(start, n_blk * QB), :]
        s = _dot_nt(qs_ref[...], kc)
        if mask is not None:
            s = jnp.where(mask, s, NEG_INF)
        s_max = s[:, 0:QB]
        for g in range(1, n_blk):
            s_max = jnp.maximum(s_max, s[:, g * QB:(g + 1) * QB])
        m_old = m_ref[...]
        m_new = jnp.maximum(m_old, jnp.max(s_max, axis=1, keepdims=True))
        alpha = jnp.exp(m_old - m_new)
        p = jnp.exp(s - m_new).astype(BF16)
        l_ref[...] = alpha * l_ref[...] + _dot(p, jnp.ones((n_blk * QB, LANES), BF16))
        acc_ref[...] = alpha * acc_ref[...] + _dot(p, kc[:, 0:C_KV_LORA])
        m_ref[...] = m_new

    def body(jp, c):
        step(2 * jp, 2, None)
        return c

    lax.fori_loop(0, i // 2, body, 0)

    @pl.when(i % 2 == 1)
    def _():
        step(i - 1, 1, None)

    step(i, 1, s_i <= t_i)
    o_lat = (acc_ref[...] * (1.0 / l_ref[:, 0:1])).astype(BF16)
    for hd in range(C_HEADS):
        o_ref[:, hd * C_V_DIM:(hd + 1) * C_V_DIM] = _dot(o_lat[hd * QB:(hd + 1) * QB], wuv_ref[hd])


def _mla_prompt(qc, kc, wuv, nb, tp):
    nt = nb * tp
    nq = tp // QB
    return pl.pallas_call(
        _mla_prompt_kernel,
        grid=(nb, nq),
        in_specs=[pl.BlockSpec((QB, C_HEADS * C_QK), lambda b, i: (b * nq + i, 0)),
                  pl.BlockSpec((tp, C_QK), lambda b, i: (b, 0)),
                  pl.BlockSpec(wuv.shape, lambda b, i: (0, 0, 0))],
        out_specs=pl.BlockSpec((QB, C_WIDTH), lambda b, i: (b * nq + i, 0)),
        out_shape=jax.ShapeDtypeStruct((nt, C_WIDTH), F32),
        scratch_shapes=[pltpu.VMEM((C_HEADS * QB, C_QK), BF16), pltpu.VMEM((C_HEADS * QB, 1), F32),
                        pltpu.VMEM((C_HEADS * QB, LANES), F32), pltpu.VMEM((C_HEADS * QB, C_KV_LORA), F32)],
        compiler_params=_cparams(("parallel", "arbitrary"), 32),
        name="mla_prompt",
    )(qc, kc, wuv)


def _mla_sample_kernel(pt_ref, q_ref, cn_ref, pn_ref, wuv_ref, cc_ref, cp_ref, o_ref, cbuf, pbuf, sem,
                       m_ref, l_ref, acc_ref, *, n_grp, n_pages):
    slot, is_last, drain = _paged_prefetch(pt_ref, (cc_ref, cp_ref), (cbuf, pbuf), sem, n_pages, n_grp, False)
    s = pl.program_id(1)
    ts = q_ref.shape[0]
    rows = C_HEADS * ts
    q = q_ref[...]
    qh = jnp.concatenate([q[:, h * C_QK:(h + 1) * C_QK] for h in range(C_HEADS)], axis=0)
    q_lat = qh[:, 0:C_KV_LORA].astype(BF16)
    q_pe = qh[:, C_KV_LORA:C_KV_LORA + C_ROPE_DIM].astype(BF16)

    def update(sc, v):
        m_old = m_ref[...]
        m_new = jnp.maximum(m_old, jnp.max(sc, axis=1, keepdims=True))
        alpha = jnp.exp(m_old - m_new)
        p = jnp.exp(sc - m_new)
        l_ref[...] = alpha * l_ref[...] + jnp.sum(p, axis=1, keepdims=True)
        acc_ref[...] = alpha * acc_ref[...] + _dot(p.astype(BF16), v)
        m_ref[...] = m_new

    @pl.when(s == 0)
    def _():
        m_ref[...] = jnp.full_like(m_ref, NEG_INF)
        l_ref[...] = jnp.zeros_like(l_ref)
        acc_ref[...] = jnp.zeros_like(acc_ref)
        cn = jnp.concatenate([cn_ref[...], jnp.zeros((PAGE - ts, C_KV_LORA), F32)], axis=0).astype(BF16)
        pn = jnp.concatenate([pn_ref[...], jnp.zeros((PAGE - ts, C_ROPE_DIM), F32)], axis=0).astype(BF16)
        t_i = lax.broadcasted_iota(I32, (rows, PAGE), 0) % ts
        s_i = lax.broadcasted_iota(I32, (rows, PAGE), 1)
        update(jnp.where(s_i <= t_i, _dot_nt(q_lat, cn) + _dot_nt(q_pe, pn), NEG_INF), cn)

    ckv = jnp.concatenate([cbuf[slot, g] for g in range(n_grp)], axis=0).astype(BF16)
    kpe_t = jnp.concatenate([pbuf[slot, g] for g in range(n_grp)], axis=1).astype(BF16)
    update(_dot_nt(q_lat, ckv) + _dot(q_pe, kpe_t), ckv)

    @pl.when(s == pl.num_programs(1) - 1)
    def _():
        o_lat = (acc_ref[...] / l_ref[...]).astype(BF16)
        for hd in range(C_HEADS):
            o_ref[:, hd * C_V_DIM:(hd + 1) * C_V_DIM] = _dot(o_lat[hd * ts:(hd + 1) * ts], wuv_ref[hd])

    @pl.when(is_last)
    def _():
        drain()


def _mla_sample(pt_flat, qc, ckv, kpe, cache_ckv, cache_kpe, wuv, n_prompt_rows, dec_b, ts, n_pages, n_grp):
    row0 = n_prompt_rows // ts
    newrow = lambda b, s, pt: (row0 + b, 0)

    grid_spec = pltpu.PrefetchScalarGridSpec(
        num_scalar_prefetch=1,
        grid=(dec_b, n_pages // n_grp),
        in_specs=[pl.BlockSpec((ts, C_HEADS * C_QK), newrow), pl.BlockSpec((ts, C_KV_LORA), newrow),
                  pl.BlockSpec((ts, C_ROPE_DIM), newrow), pl.BlockSpec(wuv.shape, lambda b, s, pt: (0, 0, 0)),
                  pl.BlockSpec(memory_space=pl.ANY), pl.BlockSpec(memory_space=pl.ANY)],
        out_specs=pl.BlockSpec((ts, C_WIDTH), lambda b, s, pt: (b, 0)),
        scratch_shapes=[pltpu.VMEM((2, n_grp, PAGE, C_KV_LORA), F32), pltpu.VMEM((2, n_grp, C_ROPE_DIM, PAGE), F32),
                        pltpu.SemaphoreType.DMA((2, 2)),
                        pltpu.VMEM((C_HEADS * ts, 1), F32), pltpu.VMEM((C_HEADS * ts, 1), F32),
                        pltpu.VMEM((C_HEADS * ts, C_KV_LORA), F32)],
    )
    return pl.pallas_call(
        functools.partial(_mla_sample_kernel, n_grp=n_grp, n_pages=n_pages),
        grid_spec=grid_spec,
        out_shape=jax.ShapeDtypeStruct((dec_b * ts, C_WIDTH), F32),
        compiler_params=_cparams(("arbitrary", "arbitrary"), 40),
        name="mla_sample",
    )(pt_flat, qc, ckv, kpe, wuv, cache_ckv, cache_kpe)


def _ln_silu(y, g, b):
    yc = y - jnp.mean(y, axis=-1, keepdims=True)
    yn = yc * lax.rsqrt(jnp.mean(yc * yc, axis=-1, keepdims=True) + LN_EPS) * g + b
    return yn * jax.nn.sigmoid(yn)


def _conv_prompt_kernel(u_ref, halo_ref, cw_ref, cb_ref, g_ref, b_ref, o_ref, ext_ref, y_ref):
    i = pl.program_id(1)
    tb = u_ref.shape[0]
    ext_ref[0:CONV_HALO, :] = jnp.where(i > 0, halo_ref[...], 0.0)
    ext_ref[CONV_HALO:, :] = u_ref[...]
    off = CONV_HALO - (CONV_W - 1)
    cw = 256
    for c0 in range(0, D_CONV, cw):
        acc = jnp.zeros((tb, cw), F32) + cb_ref[:, c0:c0 + cw]
        for k in range(CONV_W):
            acc = acc + ext_ref[off + k:off + k + tb, c0:c0 + cw] * cw_ref[k:k + 1, c0:c0 + cw]
        y_ref[:, c0:c0 + cw] = acc
    o_ref[...] = _ln_silu(y_ref[...], g_ref[...], b_ref[...])


def _conv_prompt(u, cw, cb, g, b, nb, tp):
    nt = nb * tp
    tb = QB
    nq = tp // tb
    hb = tb // CONV_HALO
    const = lambda bb, i: (0, 0)
    return pl.pallas_call(
        _conv_prompt_kernel,
        grid=(nb, nq),
        in_specs=[pl.BlockSpec((tb, D_CONV), lambda bb, i: (bb * nq + i, 0)),
                  pl.BlockSpec((CONV_HALO, D_CONV), lambda bb, i: (jnp.maximum((bb * nq + i) * hb - 1, 0), 0)),
                  pl.BlockSpec(cw.shape, const), pl.BlockSpec((1, D_CONV), const), pl.BlockSpec((1, D_CONV), const),
                  pl.BlockSpec((1, D_CONV), const)],
        out_specs=pl.BlockSpec((tb, D_CONV), lambda bb, i: (bb * nq + i, 0)),
        out_shape=jax.ShapeDtypeStruct((nt, D_CONV), F32),
        scratch_shapes=[pltpu.VMEM((CONV_HALO + tb, D_CONV), F32), pltpu.VMEM((tb, D_CONV), F32)],
        compiler_params=_cparams(("parallel", "parallel"), 32),
        name="conv_prompt",
    )(u, u, cw, cb, g, b)


def _conv_sample_kernel(ext_ref, cw_ref, cb_ref, g_ref, b_ref, o_ref):
    bb, n_ext, _ = ext_ref.shape
    ts = n_ext - (CONV_W - 1)
    acc = jnp.zeros((bb, ts, D_CONV), F32) + cb_ref[...]
    for k in range(CONV_W):
        acc = acc + ext_ref[:, k:k + ts, :] * cw_ref[k:k + 1, :]
    o_ref[...] = _ln_silu(acc.reshape(bb * ts, D_CONV), g_ref[...], b_ref[...])


def _conv_sample(ext, cw, cb, g, b, bb):
    dec_b, n_ext, _ = ext.shape
    ts = n_ext - (CONV_W - 1)
    const = lambda i: (0, 0)
    return pl.pallas_call(
        _conv_sample_kernel,
        grid=(dec_b // bb,),
        in_specs=[pl.BlockSpec((bb, n_ext, D_CONV), lambda i: (i, 0, 0)),
                  pl.BlockSpec(cw.shape, const), pl.BlockSpec((1, D_CONV), const), pl.BlockSpec((1, D_CONV), const),
                  pl.BlockSpec((1, D_CONV), const)],
        out_specs=pl.BlockSpec((bb * ts, D_CONV), lambda i: (i, 0)),
        out_shape=jax.ShapeDtypeStruct((dec_b * ts, D_CONV), F32),
        compiler_params=_cparams(("parallel",), 32),
        name="conv_sample",
    )(ext, cw, cb, g, b)


def _router_weights(router_group, router_group_bias, router_expert, router_expert_bias):
    pad = LANES - N_EXPERTS - N_GROUPS
    w = jnp.concatenate([router_expert, router_group, jnp.zeros((D_MODEL, pad), F32)], axis=1)
    b = jnp.concatenate([router_expert_bias, router_group_bias, jnp.zeros((pad,), F32)]).reshape(1, LANES)
    wh = w.astype(BF16)
    wl = (w - wh.astype(F32)).astype(BF16)
    return wh, wl, b


def _cumsum_matrix():
    j = jnp.arange(2 * QB)[:, None] % QB
    c = jnp.arange(2 * QB)[None, :]
    return jnp.where((c >= QB) | (j > c), 1.0, 0.0).astype(BF16)


def kernel(x_prompt, x_sample, cache_sb_k, cache_sb_v, state_pool, cache_mla_ckv, cache_mla_kpe, state_conv, page_table, meta_tokens, norm_mix, norm_ffn, norm_final, w_in_ab, w_pool, pool_scale, w_out_ab, w_in_cd, mla_q_norm, mla_kv_norm, w_uq, w_uk, w_uv, conv_w, conv_b, conv_norm_g, conv_norm_b, w_out_cd, router_group, router_group_bias, router_expert, router_expert_bias, expert_w_gate, expert_w_up, expert_w_down):
    nb, seq, _ = x_prompt.shape
    dec_b, ts, _ = x_sample.shape
    n_pages = page_table.shape[1]
    past_len = n_pages * PAGE
    t_real = seq + N_META
    tp = -(-t_real // QB) * QB
    n_p = nb * tp
    n_s = dec_b * ts
    nt = n_p + n_s
    tm = 256 if nt % 256 == 0 else 128
    assert ts == 8 and n_p % tm == 0 and n_s % tm == 0 and past_len >= POOL_MAX

    meta = jnp.broadcast_to(meta_tokens[None], (nb, N_META, D_MODEL))
    xp = jnp.concatenate([meta, x_prompt, jnp.zeros((nb, tp - t_real, D_MODEL), F32)], axis=1)
    x = jnp.concatenate([xp.reshape(n_p, D_MODEL), x_sample.reshape(n_s, D_MODEL)], axis=0)
    pt_flat = page_table.reshape(-1).astype(I32)
    uo = _cumsum_matrix()
    grp = next(g for g in (SB_PAGES_PER_STEP, 8, 1) if n_pages % g == 0)
    bb = 16 if dec_b % 16 == 0 else dec_b

    def prompt_rows(a, n_last):
        return a[:n_p].reshape(nb, tp, -1)[:, t_real - n_last:t_real]

    def moe_layer(layer, x_res, a_p, a_s, b_p, b_s, w_out):
        rwh, rwl, rb = _router_weights(router_group[layer], router_group_bias[layer], router_expert[layer],
                                       router_expert_bias[layer])
        x_res, hn, ri, rf, cnt = _out_proj(x_res, a_p, a_s, b_p, b_s, w_out.astype(BF16),
                                           norm_ffn[layer].reshape(1, -1), rwh, rwl, rb, tm)
        sched = _moe_schedule(ri, cnt, nt, tm)
        o = _moe(hn, *sched, expert_w_gate, expert_w_up, expert_w_down, layer)
        return x_res, o, rf

    q, k, v, u = _in_proj_ab(x, norm_mix[0].reshape(1, -1), w_in_ab.astype(BF16), tm)
    oa_p = _sb_prompt(q, k, v, uo, nb, tp)
    page_rows = (-1, PAGE * A_KV_HEADS, A_HEAD_DIM)
    oa_s = _sb_sample(pt_flat, q, k, v, cache_sb_k.reshape(page_rows), cache_sb_v.reshape(page_rows), uo, n_p, dec_b, ts,
                      n_pages, grp)
    wp = w_pool.astype(BF16)
    ps = pool_scale.reshape(1, -1)
    ob_p = _pool_prompt(u, wp, ps, nb, tp)
    pool_ext = jnp.concatenate([state_pool, u[n_p:].reshape(dec_b, ts, B_WIDTH)], axis=1)
    ob_s = _pool_sample(pool_ext, wp, ps, bb)
    x, o, rf = moe_layer(0, x, oa_p, oa_s, ob_p, ob_s, w_out_ab)

    o1 = C_Q_LORA
    o2 = o1 + C_KV_LORA
    o3 = o2 + C_ROPE_DIM
    w_cd = jnp.concatenate([w_in_cd[:, :o2], w_in_cd[:, o3:], w_in_cd[:, o2:o3],
                            jnp.zeros((D_MODEL, LANES - C_ROPE_DIM), F32)], axis=1).astype(BF16)
    wq = w_uq.reshape(C_Q_LORA, C_HEADS, C_NOPE_DIM + C_ROPE_DIM)
    wq_pe = jnp.concatenate([wq[:, :, C_NOPE_DIM:], jnp.zeros((C_Q_LORA, C_HEADS, LANES - C_ROPE_DIM), F32)], axis=2)
    wuq = jnp.concatenate([wq[:, :, :C_NOPE_DIM].reshape(C_Q_LORA, -1), wq_pe.reshape(C_Q_LORA, -1)], axis=1).astype(BF16)
    wuk = jnp.transpose(w_uk, (1, 2, 0)).astype(BF16)
    wuv = jnp.transpose(w_uv, (1, 0, 2)).astype(BF16)
    half = C_ROPE_DIM // 2
    inv_freq = ROPE_THETA ** (-jnp.arange(half, dtype=F32) / half)
    pos = jnp.concatenate([jnp.tile(jnp.arange(tp), nb), jnp.tile(past_len + jnp.arange(ts), dec_b)]).astype(F32)
    ang = pos[:, None] * inv_freq[None, :]
    zpad = jnp.zeros((nt, LANES - C_ROPE_DIM), F32)
    cos = jnp.concatenate([jnp.cos(ang), jnp.cos(ang), zpad], axis=1)
    sin = jnp.concatenate([jnp.sin(ang), jnp.sin(ang), zpad], axis=1)
    x, qc, ckv, kpe, kc, uc = _in_proj_cd(x, o, o, rf, norm_mix[1].reshape(1, -1), w_cd, mla_q_norm.reshape(1, -1),
                                          mla_kv_norm.reshape(1, -1), wuq, wuk, cos, sin, tm)
    oc_p = _mla_prompt(qc, kc, wuv, nb, tp)
    grp_c = next(g for g in (MLA_PAGES_PER_STEP, 16, 8, 1) if n_pages % g == 0)
    oc_s = _mla_sample(pt_flat, qc, ckv, kpe, cache_mla_ckv, jnp.swapaxes(cache_mla_kpe, 1, 2), wuv, n_p, dec_b, ts,
                       n_pages, grp_c)
    conv_args = (conv_w, conv_b.reshape(1, -1), conv_norm_g.reshape(1, -1), conv_norm_b.reshape(1, -1))
    od_p = _conv_prompt(uc, *conv_args, nb, tp)
    conv_ext = jnp.concatenate([state_conv, uc[n_p:].reshape(dec_b, ts, D_CONV)], axis=1)
    od_s = _conv_sample(conv_ext, *conv_args, bb)
    x, o, rf = moe_layer(1, x, oc_p, oc_s, od_p, od_s, w_out_cd)

    y = _final_norm(x, o, rf, norm_final.reshape(1, -1), tm)

    y_prompt = y[:n_p].reshape(nb, tp, D_MODEL)[:, N_META:t_real]
    y_sample = y[n_p:].reshape(dec_b, ts, D_MODEL)
    kv4 = (A_KV_HEADS, A_HEAD_DIM)
    return (y_prompt, y_sample,
            prompt_rows(k, t_real).reshape(nb, t_real, *kv4), prompt_rows(v, t_real).reshape(nb, t_real, *kv4),
            k[n_p:].reshape(dec_b, ts, *kv4), v[n_p:].reshape(dec_b, ts, *kv4),
            prompt_rows(u, POOL_MAX - 1), pool_ext[:, ts:],
            prompt_rows(ckv, t_real), prompt_rows(kpe, t_real),
            ckv[n_p:].reshape(dec_b, ts, C_KV_LORA), kpe[n_p:].reshape(dec_b, ts, C_ROPE_DIM),
            prompt_rows(uc, CONV_W - 1), conv_ext[:, ts:])
```

```python
import functools

import jax
import jax.numpy as jnp
from jax import lax
from jax.experimental import pallas as pl
from jax.experimental.pallas import tpu as pltpu

F32 = jnp.float32
BF16 = jnp.bfloat16
I32 = jnp.int32

D_MODEL = 2048
N_META = 16
RMS_EPS = 1e-6
LN_EPS = 1e-5
NEG_INF = -1e30
PAGE = 128
LANES = 128
QB = 128

A_HEADS = 8
A_KV_HEADS = 4
A_HEAD_DIM = 128
A_WIDTH = A_HEADS * A_HEAD_DIM
A_KV_WIDTH = A_KV_HEADS * A_HEAD_DIM
A_SCALE = A_HEAD_DIM ** -0.5

POOL_WINDOWS = (2, 4, 8, 16)
POOL_MAX = 16
B_WIDTH = 1024
B_GROUP_WIDTH = B_WIDTH // len(POOL_WINDOWS)

C_HEADS = 8
C_NOPE_DIM = 128
C_ROPE_DIM = 64
C_V_DIM = 128
C_Q_LORA = 512
C_KV_LORA = 256
C_WIDTH = C_HEADS * C_V_DIM
C_SCALE = (C_NOPE_DIM + C_ROPE_DIM) ** -0.5
ROPE_THETA = 10000.0
C_QK = C_KV_LORA + LANES

D_CONV = 1024
CONV_W = 31
CONV_HALO = 32

N_GROUPS = 4
EXPERTS_PER_GROUP = 8
N_EXPERTS = 32
D_EXPERT = 512
SB_PAGES_PER_STEP = 16
MLA_PAGES_PER_STEP = 64
MOE_TILE = 256
MOE_DMA_UNROLL = 8


def _cparams(sem, vmem_mb):
    return pltpu.CompilerParams(dimension_semantics=sem, vmem_limit_bytes=vmem_mb << 20)


def _rms(x, g):
    return x * lax.rsqrt(jnp.mean(x * x, axis=-1, keepdims=True) + RMS_EPS) * g


def _dot(a, b):
    return jnp.dot(a, b, preferred_element_type=F32)


def _dot_nt(a, b):
    return lax.dot_general(a, b, (((1,), (1,)), ((), ())), preferred_element_type=F32)


def _softplus(z):
    return jnp.maximum(z, 0.0) + jnp.log(1.0 + jnp.exp(-jnp.abs(z)))


def _moe_combine(x_ref, o1_ref, o2_ref, rf_ref):
    rf = rf_ref[...]
    return x_ref[...] + rf[:, 0:1] * o1_ref[...] + rf[:, 1:2] * o2_ref[...]


def _in_ab_kernel(x_ref, g_ref, w_ref, q_ref, k_ref, v_ref, u_ref):
    h = _rms(x_ref[...], g_ref[...]).astype(BF16)
    q_ref[...] = _dot(h, w_ref[:, 0:A_WIDTH]) * A_SCALE
    k_ref[...] = _dot(h, w_ref[:, A_WIDTH:A_WIDTH + A_KV_WIDTH])
    v_ref[...] = _dot(h, w_ref[:, A_WIDTH + A_KV_WIDTH:A_WIDTH + 2 * A_KV_WIDTH])
    u_ref[...] = _dot(h, w_ref[:, A_WIDTH + 2 * A_KV_WIDTH:])


def _in_proj_ab(x, g, w, tm):
    nt = x.shape[0]
    n_in = w.shape[1]
    row = lambda i: (i, 0)
    const = lambda i: (0, 0)
    return pl.pallas_call(
        _in_ab_kernel,
        grid=(nt // tm,),
        in_specs=[pl.BlockSpec((tm, D_MODEL), row), pl.BlockSpec((1, D_MODEL), const),
                  pl.BlockSpec((D_MODEL, n_in), const)],
        out_specs=[pl.BlockSpec((tm, A_WIDTH), row), pl.BlockSpec((tm, A_KV_WIDTH), row),
                   pl.BlockSpec((tm, A_KV_WIDTH), row), pl.BlockSpec((tm, B_WIDTH), row)],
        out_shape=[jax.ShapeDtypeStruct((nt, A_WIDTH), F32), jax.ShapeDtypeStruct((nt, A_KV_WIDTH), F32),
                   jax.ShapeDtypeStruct((nt, A_KV_WIDTH), F32), jax.ShapeDtypeStruct((nt, B_WIDTH), F32)],
        compiler_params=_cparams(("parallel",), 48),
        name="in_proj_ab",
    )(x, g, w)


def _rot_half(x, lane_period):
    n = x.shape[-1]
    half = C_ROPE_DIM // 2
    lane = lax.broadcasted_iota(I32, x.shape, x.ndim - 1) % lane_period
    fwd = pltpu.roll(x, n - half, x.ndim - 1)
    bwd = pltpu.roll(x, half, x.ndim - 1)
    return jnp.where(lane < half, -fwd, jnp.where(lane < C_ROPE_DIM, bwd, 0.0))


def _in_cd_kernel(x_ref, o1_ref, o2_ref, rf_ref, g_ref, w_ref, qn_ref, kvn_ref, wuq_ref, wuk_ref, cos_ref, sin_ref,
                  xo_ref, qc_ref, ckv_ref, kpe_ref, kc_ref, u_ref):
    x = _moe_combine(x_ref, o1_ref, o2_ref, rf_ref)
    xo_ref[...] = x
    h = _rms(x, g_ref[...]).astype(BF16)
    cos = cos_ref[...]
    sin = sin_ref[...]
    cq = _rms(_dot(h, w_ref[:, 0:C_Q_LORA]), qn_ref[...]).astype(BF16)
    for hd in range(C_HEADS):
        qn = _dot(cq, wuq_ref[:, hd * LANES:(hd + 1) * LANES]).astype(BF16)
        qc_ref[:, hd * C_QK:hd * C_QK + C_KV_LORA] = _dot(qn, wuk_ref[hd]) * C_SCALE
        qp = _dot(cq, wuq_ref[:, (C_HEADS + hd) * LANES:(C_HEADS + hd + 1) * LANES])
        qc_ref[:, hd * C_QK + C_KV_LORA:(hd + 1) * C_QK] = (qp * cos + _rot_half(qp, LANES) * sin) * C_SCALE
    o1 = C_Q_LORA
    o2 = o1 + C_KV_LORA
    ckv = _rms(_dot(h, w_ref[:, o1:o2]), kvn_ref[...])
    ckv_ref[...] = ckv
    o3 = o2 + 2 * D_CONV
    kr = _dot(h, w_ref[:, o3:o3 + LANES])
    kpe = kr * cos + _rot_half(kr, LANES) * sin
    kpe_ref[...] = kpe[:, 0:C_ROPE_DIM]
    kc_ref[:, 0:C_KV_LORA] = ckv.astype(BF16)
    kc_ref[:, C_KV_LORA:] = kpe.astype(BF16)
    ga = _dot(h, w_ref[:, o2:o2 + D_CONV])
    gb = _dot(h, w_ref[:, o2 + D_CONV:o3])
    u_ref[...] = ga * jax.nn.sigmoid(gb)


def _in_proj_cd(x, o1, o2, rf, g, w, qn, kvn, wuq, wuk, cos, sin, tm):
    nt = x.shape[0]
    nb = nt // tm
    row = lambda i: (i, 0)
    row2 = lambda i: (i + nb, 0)
    const = lambda i: (0, 0)
    return pl.pallas_call(
        _in_cd_kernel,
        grid=(nb,),
        in_specs=[pl.BlockSpec((tm, D_MODEL), row), pl.BlockSpec((tm, D_MODEL), row), pl.BlockSpec((tm, D_MODEL), row2),
                  pl.BlockSpec((tm, LANES), row), pl.BlockSpec((1, D_MODEL), const),
                  pl.BlockSpec(w.shape, const), pl.BlockSpec((1, C_Q_LORA), const), pl.BlockSpec((1, C_KV_LORA), const),
                  pl.BlockSpec(wuq.shape, const), pl.BlockSpec(wuk.shape, lambda i: (0, 0, 0)),
                  pl.BlockSpec((tm, LANES), row), pl.BlockSpec((tm, LANES), row)],
        out_specs=[pl.BlockSpec((tm, D_MODEL), row), pl.BlockSpec((tm, C_HEADS * C_QK), row),
                   pl.BlockSpec((tm, C_KV_LORA), row), pl.BlockSpec((tm, C_ROPE_DIM), row),
                   pl.BlockSpec((tm, C_QK), row), pl.BlockSpec((tm, D_CONV), row)],
        out_shape=[jax.ShapeDtypeStruct((nt, D_MODEL), F32), jax.ShapeDtypeStruct((nt, C_HEADS * C_QK), F32),
                   jax.ShapeDtypeStruct((nt, C_KV_LORA), F32), jax.ShapeDtypeStruct((nt, C_ROPE_DIM), F32),
                   jax.ShapeDtypeStruct((nt, C_QK), BF16), jax.ShapeDtypeStruct((nt, D_CONV), F32)],
        compiler_params=_cparams(("parallel",), 56),
        name="in_proj_cd",
    )(x, o1, o2, rf, g, w, qn, kvn, wuq, wuk, cos, sin)


def _out_proj_kernel(x_ref, ap_ref, as_ref, bp_ref, bs_ref, w_ref, g_ref, rwh_ref, rwl_ref, rb_ref,
                     xo_ref, hn_ref, ri_ref, rf_ref, cnt_ref, carry_ref, *, n_prompt_tiles):
    tm = x_ref.shape[0]
    step = pl.program_id(0)

    @pl.when(step == 0)
    def _():
        carry_ref[...] = jnp.zeros_like(carry_ref)

    is_prompt = step < n_prompt_tiles
    a = jnp.where(is_prompt, ap_ref[...], as_ref[...])
    b = jnp.where(is_prompt, bp_ref[...], bs_ref[...])
    ab = jnp.concatenate([a, b], axis=1).astype(BF16)
    x = x_ref[...] + _dot(ab, w_ref[...])
    xo_ref[...] = x
    hn = _rms(x, g_ref[...])
    hn_ref[...] = hn
    hh = hn.astype(BF16)
    hl = (hn - hh.astype(F32)).astype(BF16)
    logit = _dot(hh, rwh_ref[...]) + _dot(hl, rwh_ref[...]) + _dot(hh, rwl_ref[...]) + rb_ref[...]
    lane = lax.broadcasted_iota(I32, (tm, LANES), 1)
    lanef = lane.astype(F32)
    big = 1e9
    is_g = (lane >= N_EXPERTS) & (lane < N_EXPERTS + N_GROUPS)
    gl = jnp.where(is_g, logit, NEG_INF)
    gmax = jnp.max(gl, axis=1, keepdims=True)
    grp_lane = jnp.min(jnp.where(gl == gmax, lanef, big), axis=1, keepdims=True)
    g_w = 1.0 / jnp.sum(jnp.where(is_g, jnp.exp(gl - gmax), 0.0), axis=1, keepdims=True)
    grp = grp_lane.astype(I32) - N_EXPERTS
    in_grp = (lane < N_EXPERTS) & (jnp.right_shift(lane, 3) == grp)
    el = jnp.where(in_grp, logit, NEG_INF)
    v1 = jnp.max(el, axis=1, keepdims=True)
    i1 = jnp.min(jnp.where(el == v1, lanef, big), axis=1, keepdims=True)
    el2 = jnp.where(lanef == i1, NEG_INF, el)
    v2 = jnp.max(el2, axis=1, keepdims=True)
    i2 = jnp.min(jnp.where(el2 == v2, lanef, big), axis=1, keepdims=True)
    e21 = jnp.exp(v2 - v1)
    gate1 = g_w / (1.0 + e21)
    gate2 = gate1 * e21
    oh1 = lanef == i1
    oh2 = lanef == i2
    cf = jnp.where(oh1 | oh2, 1.0, 0.0)
    r_i = lax.broadcasted_iota(I32, (tm, tm), 0)
    c_i = lax.broadcasted_iota(I32, (tm, tm), 1)
    tri = jnp.where(r_i > c_i, 1.0, 0.0).astype(BF16)
    carry = carry_ref[...]
    cum = _dot(tri, cf.astype(BF16)) + carry[0:1, :]
    rank1 = jnp.sum(jnp.where(oh1, cum, 0.0), axis=1, keepdims=True)
    rank2 = jnp.sum(jnp.where(oh2, cum, 0.0), axis=1, keepdims=True)
    carry = carry + jnp.sum(cf, axis=0, keepdims=True)
    carry_ref[...] = carry
    cnt_ref[...] = carry
    ri = jnp.where(lane == 0, i1, jnp.where(lane == 1, i2, jnp.where(lane == 2, rank1, jnp.where(lane == 3, rank2, 0.0))))
    ri_ref[...] = ri.astype(I32)
    rf_ref[...] = jnp.where(lane == 0, gate1, jnp.where(lane == 1, gate2, 0.0))


def _out_proj(x, a_p, a_s, b_p, b_s, w, g, rwh, rwl, rb, tm):
    nt = x.shape[0]
    nbp = a_p.shape[0] // tm
    row = lambda i: (i, 0)
    prow = lambda i: (jnp.minimum(i, nbp - 1), 0)
    srow = lambda i: (jnp.maximum(i - nbp, 0), 0)
    const = lambda i: (0, 0)
    return pl.pallas_call(
        functools.partial(_out_proj_kernel, n_prompt_tiles=nbp),
        grid=(nt // tm,),
        in_specs=[pl.BlockSpec((tm, D_MODEL), row), pl.BlockSpec((tm, a_p.shape[1]), prow),
                  pl.BlockSpec((tm, a_s.shape[1]), srow), pl.BlockSpec((tm, b_p.shape[1]), prow),
                  pl.BlockSpec((tm, b_s.shape[1]), srow),
                  pl.BlockSpec(w.shape, const), pl.BlockSpec((1, D_MODEL), const),
                  pl.BlockSpec((D_MODEL, LANES), const), pl.BlockSpec((D_MODEL, LANES), const), pl.BlockSpec((1, LANES), const)],
        out_specs=[pl.BlockSpec((tm, D_MODEL), row), pl.BlockSpec((tm, D_MODEL), row),
                   pl.BlockSpec((tm, LANES), row), pl.BlockSpec((tm, LANES), row), pl.BlockSpec((8, LANES), const)],
        out_shape=[jax.ShapeDtypeStruct((nt, D_MODEL), F32), jax.ShapeDtypeStruct((nt, D_MODEL), F32),
                   jax.ShapeDtypeStruct((nt, LANES), I32), jax.ShapeDtypeStruct((nt, LANES), F32),
                   jax.ShapeDtypeStruct((8, LANES), F32)],
        scratch_shapes=[pltpu.VMEM((8, LANES), F32)],
        compiler_params=_cparams(("arbitrary",), 48),
        name="out_proj_router",
    )(x, a_p, a_s, b_p, b_s, w, g, rwh, rwl, rb)


def _final_norm_kernel(x_ref, o1_ref, o2_ref, rf_ref, g_ref, y_ref):
    y_ref[...] = _rms(_moe_combine(x_ref, o1_ref, o2_ref, rf_ref), g_ref[...])


def _final_norm(x, o, rf, g, tm):
    nt = x.shape[0]
    nb = nt // tm
    row = lambda i: (i, 0)
    return pl.pallas_call(
        _final_norm_kernel,
        grid=(nb,),
        in_specs=[pl.BlockSpec((tm, D_MODEL), row), pl.BlockSpec((tm, D_MODEL), row),
                  pl.BlockSpec((tm, D_MODEL), lambda i: (i + nb, 0)), pl.BlockSpec((tm, LANES), row),
                  pl.BlockSpec((1, D_MODEL), lambda i: (0, 0))],
        out_specs=pl.BlockSpec((tm, D_MODEL), row),
        out_shape=jax.ShapeDtypeStruct((nt, D_MODEL), F32),
        compiler_params=_cparams(("parallel",), 32),
        name="final_norm",
    )(x, o, o, rf, g)


def _moe_kernel(te_ref, nu_ref, tv_ref, did_ref, hn_ref, wg_ref, wu_ref, wd_ref, o_ref,
                xbuf, ybuf, gsem, ssem):
    i = pl.program_id(0)
    n_used = nu_ref[0]
    tmm = ybuf.shape[0]
    n_tok = hn_ref.shape[0]

    def gather(tile, slot):
        base = tile * tmm

        def body(c8, c):
            for u in range(MOE_DMA_UNROLL):
                r = c8 * MOE_DMA_UNROLL + u
                did = did_ref[base + r]
                rid = jnp.where(did >= n_tok, did - n_tok, did)
                pltpu.make_async_copy(hn_ref.at[pl.ds(rid, 1), :], xbuf.at[slot, pl.ds(r, 1), :], gsem.at[slot]).start()
            return c

        lax.fori_loop(0, tmm // MOE_DMA_UNROLL, body, 0)

    def gather_wait(slot):
        pltpu.make_async_copy(hn_ref.at[pl.ds(0, tmm), :], xbuf.at[slot], gsem.at[slot]).wait()

    def scatter_wait(tile):
        n_valid = tv_ref[tile]

        @pl.when(n_valid == tmm)
        def _():
            pltpu.make_async_copy(ybuf, o_ref.at[pl.ds(0, tmm), :], ssem.at[0]).wait()

        @pl.when(n_valid < tmm)
        def _():
            def body(r, c):
                pltpu.make_async_copy(ybuf.at[pl.ds(0, 1), :], o_ref.at[pl.ds(0, 1), :], ssem.at[0]).wait()
                return c

            lax.fori_loop(0, n_valid, body, 0)

    @pl.when(i == 0)
    def _():
        gather(0, 0)

    @pl.when(i < n_used)
    def _():
        slot = i % 2
        gather_wait(slot)
        x = xbuf[slot].astype(BF16)
        base_next = jnp.minimum(i + 1, n_used - 1) * tmm
        for r in range(tmm):
            did = did_ref[base_next + r]
            rid = jnp.where(did >= n_tok, did - n_tok, did)
            pltpu.make_async_copy(hn_ref.at[pl.ds(rid, 1), :], xbuf.at[1 - slot, pl.ds(r, 1), :],
                                  gsem.at[1 - slot]).start()
        hid = jax.nn.silu(_dot(x, wg_ref[...].astype(BF16))) * _dot(x, wu_ref[...].astype(BF16))
        y = _dot(hid.astype(BF16), wd_ref[...].astype(BF16))

        @pl.when(i > 0)
        def _():
            scatter_wait(i - 1)

        ybuf[...] = y
        base = i * tmm

        def scatter_row(r):
            did = did_ref[base + r]
            pltpu.make_async_copy(ybuf.at[pl.ds(r, 1), :], o_ref.at[pl.ds(did, 1), :], ssem.at[0]).start()

        def body_unrolled(c8, c):
            for u in range(MOE_DMA_UNROLL):
                scatter_row(c8 * MOE_DMA_UNROLL + u)
            return c

        def body_single(r, c):
            scatter_row(r)
            return c

        n_valid = tv_ref[i]
        n_chunks = n_valid // MOE_DMA_UNROLL
        lax.fori_loop(0, n_chunks, body_unrolled, 0)
        lax.fori_loop(n_chunks * MOE_DMA_UNROLL, n_valid, body_single, 0)

        @pl.when(i == n_used - 1)
        def _():
            scatter_wait(i)
            gather_wait(1 - slot)


def _moe(hn, tile_expert, n_used, tile_valid, dst_ids, w_gate, w_up, w_down, layer):
    nt = hn.shape[0]
    n_tiles = tile_expert.shape[0]
    wspec = lambda shp: pl.BlockSpec((None, None) + shp, lambda i, te, nu, tv, did: (layer, te[i], 0, 0))
    grid_spec = pltpu.PrefetchScalarGridSpec(
        num_scalar_prefetch=4,
        grid=(n_tiles,),
        in_specs=[pl.BlockSpec(memory_space=pl.ANY), wspec((D_MODEL, D_EXPERT)), wspec((D_MODEL, D_EXPERT)),
                  wspec((D_EXPERT, D_MODEL))],
        out_specs=pl.BlockSpec(memory_space=pl.ANY),
        scratch_shapes=[pltpu.VMEM((2, MOE_TILE, D_MODEL), F32), pltpu.VMEM((MOE_TILE, D_MODEL), F32),
                        pltpu.SemaphoreType.DMA((2,)), pltpu.SemaphoreType.DMA((1,))],
    )
    return pl.pallas_call(
        _moe_kernel,
        grid_spec=grid_spec,
        out_shape=jax.ShapeDtypeStruct((2 * nt, D_MODEL), F32),
        compiler_params=_cparams(("arbitrary",), 56),
        name="moe_experts",
    )(tile_expert, n_used, tile_valid, dst_ids, hn, w_gate, w_up, w_down)


def _moe_pos_kernel(ri_ref, offs_ref, pos_ref):
    ri = ri_ref[...]
    lane = lax.broadcasted_iota(I32, ri.shape, 1)
    offs = offs_ref[...]
    pos1 = jnp.sum(jnp.where(lane == ri[:, 0:1], offs, 0.0), axis=1, keepdims=True).astype(I32) + ri[:, 2:3]
    pos2 = jnp.sum(jnp.where(lane == ri[:, 1:2], offs, 0.0), axis=1, keepdims=True).astype(I32) + ri[:, 3:4]
    pos_ref[...] = jnp.where(lane == 0, pos1, jnp.where(lane == 1, pos2, 0))


def _moe_positions(ri, offs, tm):
    nt = ri.shape[0]
    offs_row = jnp.zeros((1, LANES), F32).at[0, :N_EXPERTS].set(offs.astype(F32))
    return pl.pallas_call(
        _moe_pos_kernel,
        grid=(nt // tm,),
        in_specs=[pl.BlockSpec((tm, LANES), lambda i: (i, 0)), pl.BlockSpec((1, LANES), lambda i: (0, 0))],
        out_specs=pl.BlockSpec((tm, LANES), lambda i: (i, 0)),
        out_shape=jax.ShapeDtypeStruct((nt, LANES), I32),
        compiler_params=_cparams(("parallel",), 32),
        name="moe_positions",
    )(ri, offs_row)


def _moe_schedule(ri, cnt, nt, tm):
    counts = cnt[0, :N_EXPERTS].astype(I32)
    padded = ((counts + MOE_TILE - 1) // MOE_TILE) * MOE_TILE
    ends = jnp.cumsum(padded)
    offs = ends - padded
    n_tiles = -(-2 * nt // MOE_TILE) + N_EXPERTS
    n_used = (ends[-1] // MOE_TILE).astype(I32)
    tile_raw = jnp.arange(n_tiles, dtype=I32)
    tile = jnp.minimum(tile_raw, n_used - 1)
    tile_expert = jnp.minimum(jnp.sum((ends // MOE_TILE)[None, :] <= tile[:, None], axis=1), N_EXPERTS - 1).astype(I32)
    rows_left = counts[tile_expert] - (tile * MOE_TILE - offs[tile_expert])
    tile_valid = jnp.where(tile_raw < n_used, jnp.clip(rows_left, 0, MOE_TILE), 0).astype(I32)
    pos = _moe_positions(ri, offs, tm)
    n_sorted = n_tiles * MOE_TILE
    dst_ids = jnp.zeros((n_sorted,), I32).at[jnp.concatenate([pos[:, 0], pos[:, 1]])].set(
        jnp.arange(2 * nt, dtype=I32), unique_indices=True)
    return tile_expert, n_used.reshape(1), tile_valid, dst_ids


def _sb_blocks(q, kt, vt, uo, acc, mask):
    n_h = len(q)
    rows = q[0].shape[0]
    n_blk = kt[0].shape[0] // QB
    z = jnp.concatenate([_dot_nt(q[h], kt[h]) for h in range(n_h)], axis=0)
    sp = _softplus(z)
    lk = -sp
    if mask is not None:
        lk = jnp.where(mask, lk, 0.0)
    hi = lk.astype(BF16)
    lo = (lk - hi.astype(F32)).astype(BF16)
    split = jnp.concatenate([jnp.concatenate([hi[:, g * QB:(g + 1) * QB], lo[:, g * QB:(g + 1) * QB]], axis=1)
                             for g in range(n_blk)], axis=0)
    cs = _dot(split, uo)
    later = []
    for g in range(n_blk):
        cs_g = cs[g * n_h * rows:(g + 1) * n_h * rows]
        later.append(cs_g[:, 0:QB] + acc)
        acc = acc + cs_g[:, QB:]
    w = jnp.exp(z - sp + jnp.concatenate(later, axis=1))
    if mask is not None:
        w = jnp.where(mask, w, 0.0)
    w = w.astype(BF16)
    out = jnp.concatenate([_dot(w[h * rows:(h + 1) * rows], vt[h]) for h in range(n_h)], axis=0)
    return out, acc


def _sb_prompt_kernel(q_ref, k_ref, v_ref, uo_ref, o_ref, acc_ref, out_ref):
    i = pl.program_id(1)
    group = A_HEADS // A_KV_HEADS
    gw = group * A_HEAD_DIM
    q = [jnp.concatenate([q_ref[:, h * gw + g * A_HEAD_DIM:h * gw + (g + 1) * A_HEAD_DIM] for g in range(group)],
                         axis=0).astype(BF16) for h in range(A_KV_HEADS)]
    uo = uo_ref[...]
    t_i = lax.broadcasted_iota(I32, (A_HEADS * QB, QB), 0) % QB
    s_i = lax.broadcasted_iota(I32, (A_HEADS * QB, QB), 1)

    def step(j, mask):
        start = pl.multiple_of(j * QB, QB)
        kt = [k_ref[pl.ds(start, QB), h * A_HEAD_DIM:(h + 1) * A_HEAD_DIM].astype(BF16) for h in range(A_KV_HEADS)]
        vt = [v_ref[pl.ds(start, QB), h * A_HEAD_DIM:(h + 1) * A_HEAD_DIM].astype(BF16) for h in range(A_KV_HEADS)]
        return _sb_blocks(q, kt, vt, uo, acc_ref[...], mask)

    acc_ref[...] = jnp.zeros_like(acc_ref)
    out, acc = step(i, s_i < t_i)
    out_ref[...] = out
    acc_ref[...] = acc

    def body(jj, c):
        out, acc = step(i - 1 - jj, None)
        out_ref[...] += out
        acc_ref[...] = acc
        return c

    lax.fori_loop(0, i, body, 0)
    for hh in range(A_HEADS):
        o_ref[:, hh * A_HEAD_DIM:(hh + 1) * A_HEAD_DIM] = out_ref[hh * QB:(hh + 1) * QB, :]


def _sb_prompt(q, k, v, uo, nb, tp):
    nt = nb * tp
    nq = tp // QB
    return pl.pallas_call(
        _sb_prompt_kernel,
        grid=(nb, nq),
        in_specs=[pl.BlockSpec((QB, A_WIDTH), lambda b, i: (b * nq + i, 0)),
                  pl.BlockSpec((tp, A_KV_WIDTH), lambda b, i: (b, 0)),
                  pl.BlockSpec((tp, A_KV_WIDTH), lambda b, i: (b, 0)),
                  pl.BlockSpec((2 * QB, 2 * QB), lambda b, i: (0, 0))],
        out_specs=pl.BlockSpec((QB, A_WIDTH), lambda b, i: (b * nq + i, 0)),
        out_shape=jax.ShapeDtypeStruct((nt, A_WIDTH), F32),
        scratch_shapes=[pltpu.VMEM((A_HEADS * QB, QB), F32), pltpu.VMEM((A_HEADS * QB, QB), F32)],
        compiler_params=_cparams(("parallel", "arbitrary"), 40),
        name="sb_prompt",
    )(q, k, v, uo)


def _paged_prefetch(pt_ref, caches, bufs, sem, n_pages, n_grp, newest_first):
    n_steps = pl.num_programs(1)
    t = pl.program_id(0) * n_steps + pl.program_id(1)
    last = pl.num_programs(0) * n_steps - 1
    slot = t % 2

    def start(step, sl):
        seq = step // n_steps
        first = (step % n_steps) * n_grp
        for g in range(n_grp):
            idx = (n_pages - 1 - (first + g)) if newest_first else (first + g)
            page = pt_ref[seq * n_pages + idx]
            for k, (cache, buf) in enumerate(zip(caches, bufs)):
                pltpu.make_async_copy(cache.at[page], buf.at[sl, g], sem.at[k, sl]).start()

    def wait(sl):
        for k, (cache, buf) in enumerate(zip(caches, bufs)):
            pltpu.make_async_copy(cache.at[pl.ds(0, n_grp)], buf.at[sl], sem.at[k, sl]).wait()

    @pl.when(t == 0)
    def _():
        start(0, 0)

    start(jnp.minimum(t + 1, last), 1 - slot)
    wait(slot)
    return slot, t == last, lambda: wait(1 - slot)


def _sb_sample_kernel(pt_ref, q_ref, kn_ref, vn_ref, uo_ref, ck_ref, cv_ref, o_ref, kbuf, vbuf, sem, acc_ref, out_ref,
                      *, n_grp, n_pages):
    slot, is_last, drain = _paged_prefetch(pt_ref, (ck_ref, cv_ref), (kbuf, vbuf), sem, n_pages, n_grp, True)
    s = pl.program_id(1)
    ts = q_ref.shape[0]
    group = A_HEADS // A_KV_HEADS
    gw = group * A_HEAD_DIM
    qf = q_ref[...]
    q = [jnp.concatenate([qf[:, h * gw + g * A_HEAD_DIM:h * gw + (g + 1) * A_HEAD_DIM] for g in range(group)],
                         axis=0).astype(BF16) for h in range(A_KV_HEADS)]
    uo = uo_ref[...]
    n_rows = A_HEADS * ts

    @pl.when(s == 0)
    def _():
        pad = jnp.zeros((QB - ts, A_KV_WIDTH), F32)
        kn = jnp.concatenate([kn_ref[...], pad], axis=0)
        vn = jnp.concatenate([vn_ref[...], pad], axis=0)
        heads = lambda x: [x[:, h * A_HEAD_DIM:(h + 1) * A_HEAD_DIM].astype(BF16) for h in range(A_KV_HEADS)]
        t_i = lax.broadcasted_iota(I32, (n_rows, QB), 0) % ts
        s_i = lax.broadcasted_iota(I32, (n_rows, QB), 1)
        out, acc = _sb_blocks(q, heads(kn), heads(vn), uo, jnp.zeros((n_rows, QB), F32), s_i < t_i)
        out_ref[...] = out
        acc_ref[...] = acc

    def head_rows(buf, g, h):
        return buf[slot, g, pl.ds(h, PAGE, stride=A_KV_HEADS), :]

    kt = [jnp.concatenate([head_rows(kbuf, g, h) for g in range(n_grp)], axis=0).astype(BF16)
          for h in range(A_KV_HEADS)]
    vt = [jnp.concatenate([head_rows(vbuf, g, h) for g in range(n_grp)], axis=0).astype(BF16)
          for h in range(A_KV_HEADS)]
    out, acc = _sb_blocks(q, kt, vt, uo, acc_ref[...], None)
    out = out_ref[...] + out
    out_ref[...] = out
    acc_ref[...] = acc

    @pl.when(s == pl.num_programs(1) - 1)
    def _():
        for h in range(A_HEADS):
            o_ref[:, h * A_HEAD_DIM:(h + 1) * A_HEAD_DIM] = out[h * ts:(h + 1) * ts, :]

    @pl.when(is_last)
    def _():
        drain()


def _sb_sample(pt_flat, q, k, v, cache_k, cache_v, uo, n_prompt_rows, dec_b, ts, n_pages, n_grp):
    row0 = n_prompt_rows // ts
    newrow = lambda b, s, pt: (row0 + b, 0)

    page_shape = (PAGE * A_KV_HEADS, A_HEAD_DIM)
    grid_spec = pltpu.PrefetchScalarGridSpec(
        num_scalar_prefetch=1,
        grid=(dec_b, n_pages // n_grp),
        in_specs=[pl.BlockSpec((ts, A_WIDTH), newrow), pl.BlockSpec((ts, A_KV_WIDTH), newrow),
                  pl.BlockSpec((ts, A_KV_WIDTH), newrow), pl.BlockSpec((2 * QB, 2 * QB), lambda b, s, pt: (0, 0)),
                  pl.BlockSpec(memory_space=pl.ANY), pl.BlockSpec(memory_space=pl.ANY)],
        out_specs=pl.BlockSpec((ts, A_WIDTH), lambda b, s, pt: (b, 0)),
        scratch_shapes=[pltpu.VMEM((2, n_grp) + page_shape, F32), pltpu.VMEM((2, n_grp) + page_shape, F32),
                        pltpu.SemaphoreType.DMA((2, 2)),
                        pltpu.VMEM((A_HEADS * ts, QB), F32), pltpu.VMEM((A_HEADS * ts, QB), F32)],
    )
    return pl.pallas_call(
        functools.partial(_sb_sample_kernel, n_grp=n_grp, n_pages=n_pages),
        grid_spec=grid_spec,
        out_shape=jax.ShapeDtypeStruct((dec_b * ts, A_WIDTH), F32),
        compiler_params=_cparams(("arbitrary", "arbitrary"), 40),
        name="sb_sample",
    )(pt_flat, q, k, v, uo, cache_k, cache_v)


def _pool_project(d, gi, wp_ref, ps_ref):
    c0 = gi * B_GROUP_WIDTH
    return _dot(d.astype(BF16), wp_ref[gi]) * ps_ref[:, c0:c0 + B_GROUP_WIDTH]


def _pool_prompt_kernel(u_ref, halo_ref, wp_ref, ps_ref, o_ref, ext_ref):
    i = pl.program_id(1)
    tb = u_ref.shape[0]
    ext_ref[0:POOL_MAX, :] = jnp.where(i > 0, halo_ref[...], 0.0)
    ext_ref[POOL_MAX:, :] = u_ref[...]
    pos = (i * tb + lax.broadcasted_iota(I32, (tb, 1), 0)).astype(F32)
    for gi, win in enumerate(POOL_WINDOWS):
        c0 = gi * B_GROUP_WIDTH
        c1 = c0 + B_GROUP_WIDTH
        cur = ext_ref[POOL_MAX:, c0:c1]
        ws = cur
        for k in range(1, win):
            ws = ws + ext_ref[POOL_MAX - k:POOL_MAX - k + tb, c0:c1]
        d = ws * (1.0 / jnp.minimum(pos + 1.0, float(win))) - cur
        o_ref[:, c0:c1] = _pool_project(d, gi, wp_ref, ps_ref)


def _pool_prompt(u, wp, ps, nb, tp):
    nt = nb * tp
    tb = QB
    nq = tp // tb
    hb = tb // POOL_MAX
    return pl.pallas_call(
        _pool_prompt_kernel,
        grid=(nb, nq),
        in_specs=[pl.BlockSpec((tb, B_WIDTH), lambda b, i: (b * nq + i, 0)),
                  pl.BlockSpec((POOL_MAX, B_WIDTH), lambda b, i: (jnp.maximum((b * nq + i) * hb - 1, 0), 0)),
                  pl.BlockSpec(wp.shape, lambda b, i: (0, 0, 0)), pl.BlockSpec((1, B_WIDTH), lambda b, i: (0, 0))],
        out_specs=pl.BlockSpec((tb, B_WIDTH), lambda b, i: (b * nq + i, 0)),
        out_shape=jax.ShapeDtypeStruct((nt, B_WIDTH), F32),
        scratch_shapes=[pltpu.VMEM((POOL_MAX + tb, B_WIDTH), F32)],
        compiler_params=_cparams(("parallel", "parallel"), 32),
        name="pool_prompt",
    )(u, u, wp, ps)


def _pool_sample_kernel(ext_ref, wp_ref, ps_ref, o_ref):
    bb, n_ext, _ = ext_ref.shape
    ts = n_ext - (POOL_MAX - 1)
    for gi, win in enumerate(POOL_WINDOWS):
        c0 = gi * B_GROUP_WIDTH
        c1 = c0 + B_GROUP_WIDTH
        cur = ext_ref[:, POOL_MAX - 1:, c0:c1]
        ws = cur
        for k in range(1, win):
            ws = ws + ext_ref[:, POOL_MAX - 1 - k:POOL_MAX - 1 - k + ts, c0:c1]
        d = (ws * (1.0 / win) - cur).reshape(bb * ts, B_GROUP_WIDTH)
        o_ref[:, c0:c1] = _pool_project(d, gi, wp_ref, ps_ref)


def _pool_sample(ext, wp, ps, bb):
    dec_b, n_ext, _ = ext.shape
    ts = n_ext - (POOL_MAX - 1)
    return pl.pallas_call(
        _pool_sample_kernel,
        grid=(dec_b // bb,),
        in_specs=[pl.BlockSpec((bb, n_ext, B_WIDTH), lambda i: (i, 0, 0)),
                  pl.BlockSpec(wp.shape, lambda i: (0, 0, 0)), pl.BlockSpec((1, B_WIDTH), lambda i: (0, 0))],
        out_specs=pl.BlockSpec((bb * ts, B_WIDTH), lambda i: (i, 0)),
        out_shape=jax.ShapeDtypeStruct((dec_b * ts, B_WIDTH), F32),
        compiler_params=_cparams(("parallel",), 32),
        name="pool_sample",
    )(ext, wp, ps)


def _mla_prompt_kernel(q_ref, kc_ref, wuv_ref, o_ref, qs_ref, m_ref, l_ref, acc_ref):
    i = pl.program_id(1)
    n_rows = C_HEADS * QB
    t_i = lax.broadcasted_iota(I32, (n_rows, QB), 0) % QB
    s_i = lax.broadcasted_iota(I32, (n_rows, QB), 1)
    for hd in range(C_HEADS):
        qs_ref[hd * QB:(hd + 1) * QB, :] = q_ref[:, hd * C_QK:(hd + 1) * C_QK].astype(BF16)
    m_ref[...] = jnp.full_like(m_ref, NEG_INF)
    l_ref[...] = jnp.zeros_like(l_ref)
    acc_ref[...] = jnp.zeros_like(acc_ref)

    def step(j, n_blk, mask):
        start = pl.multiple_of(j * QB, QB)
        kc = kc_ref[pl.ds(start, n_blk * QB), :]
        s = _dot_nt(qs_ref[...], kc)
        if mask is not None:
            s = jnp.where(mask, s, NEG_INF)
        s_max = s[:, 0:QB]
        for g in range(1, n_blk):
            s_max = jnp.maximum(s_max, s[:, g * QB:(g + 1) * QB])
        m_old = m_ref[...]
        m_new = jnp.maximum(m_old, jnp.max(s_max, axis=1, keepdims=True))
        alpha = jnp.exp(m_old - m_new)
        p = jnp.exp(s - m_new).astype(BF16)
        l_ref[...] = alpha * l_ref[...] + _dot(p, jnp.ones((n_blk * QB, LANES), BF16))
        acc_ref[...] = alpha * acc_ref[...] + _dot(p, kc[:, 0:C_KV_LORA])
        m_ref[...] = m_new

    def body(jp, c):
        step(2 * jp, 2, None)
        return c

    lax.fori_loop(0, i // 2, body, 0)

    @pl.when(i % 2 == 1)
    def _():
        step(i - 1, 1, None)

    step(i, 1, s_i <= t_i)
    o_lat = (acc_ref[...] * (1.0 / l_ref[:, 0:1])).astype(BF16)
    for hd in range(C_HEADS):
        o_ref[:, hd * C_V_DIM:(hd + 1) * C_V_DIM] = _dot(o_lat[hd * QB:(hd + 1) * QB], wuv_ref[hd])


def _mla_prompt(qc, kc, wuv, nb, tp):
    nt = nb * tp
    nq = tp // QB
    return pl.pallas_call(
        _mla_prompt_kernel,
        grid=(nb, nq),
        in_specs=[pl.BlockSpec((QB, C_HEADS * C_QK), lambda b, i: (b * nq + i, 0)),
                  pl.BlockSpec((tp, C_QK), lambda b, i: (b, 0)),
                  pl.BlockSpec(wuv.shape, lambda b, i: (0, 0, 0))],
        out_specs=pl.BlockSpec((QB, C_WIDTH), lambda b, i: (b * nq + i, 0)),
        out_shape=jax.ShapeDtypeStruct((nt, C_WIDTH), F32),
        scratch_shapes=[pltpu.VMEM((C_HEADS * QB, C_QK), BF16), pltpu.VMEM((C_HEADS * QB, 1), F32),
                        pltpu.VMEM((C_HEADS * QB, LANES), F32), pltpu.VMEM((C_HEADS * QB, C_KV_LORA), F32)],
        compiler_params=_cparams(("parallel", "arbitrary"), 32),
        name="mla_prompt",
    )(qc, kc, wuv)


def _mla_sample_kernel(pt_ref, q_ref, cn_ref, pn_ref, wuv_ref, cc_ref, cp_ref, o_ref, cbuf, pbuf, sem,
                       m_ref, l_ref, acc_ref, *, n_grp, n_pages):
    slot, is_last, drain = _paged_prefetch(pt_ref, (cc_ref, cp_ref), (cbuf, pbuf), sem, n_pages, n_grp, False)
    s = pl.program_id(1)
    ts = q_ref.shape[0]
    rows = C_HEADS * ts
    q = q_ref[...]
    qh = jnp.concatenate([q[:, h * C_QK:(h + 1) * C_QK] for h in range(C_HEADS)], axis=0)
    q_lat = qh[:, 0:C_KV_LORA].astype(BF16)
    q_pe = qh[:, C_KV_LORA:C_KV_LORA + C_ROPE_DIM].astype(BF16)

    def update(sc, v):
        m_old = m_ref[...]
        m_new = jnp.maximum(m_old, jnp.max(sc, axis=1, keepdims=True))
        alpha = jnp.exp(m_old - m_new)
        p = jnp.exp(sc - m_new)
        l_ref[...] = alpha * l_ref[...] + jnp.sum(p, axis=1, keepdims=True)
        acc_ref[...] = alpha * acc_ref[...] + _dot(p.astype(BF16), v)
        m_ref[...] = m_new

    @pl.when(s == 0)
    def _():
        m_ref[...] = jnp.full_like(m_ref, NEG_INF)
        l_ref[...] = jnp.zeros_like(l_ref)
        acc_ref[...] = jnp.zeros_like(acc_ref)
        cn = jnp.concatenate([cn_ref[...], jnp.zeros((PAGE - ts, C_KV_LORA), F32)], axis=0).astype(BF16)
        pn = jnp.concatenate([pn_ref[...], jnp.zeros((PAGE - ts, C_ROPE_DIM), F32)], axis=0).astype(BF16)
        t_i = lax.broadcasted_iota(I32, (rows, PAGE), 0) % ts
        s_i = lax.broadcasted_iota(I32, (rows, PAGE), 1)
        update(jnp.where(s_i <= t_i, _dot_nt(q_lat, cn) + _dot_nt(q_pe, pn), NEG_INF), cn)

    ckv = jnp.concatenate([cbuf[slot, g] for g in range(n_grp)], axis=0).astype(BF16)
    kpe_t = jnp.concatenate([pbuf[slot, g] for g in range(n_grp)], axis=1).astype(BF16)
    update(_dot_nt(q_lat, ckv) + _dot(q_pe, kpe_t), ckv)

    @pl.when(s == pl.num_programs(1) - 1)
    def _():
        o_lat = (acc_ref[...] / l_ref[...]).astype(BF16)
        for hd in range(C_HEADS):
            o_ref[:, hd * C_V_DIM:(hd + 1) * C_V_DIM] = _dot(o_lat[hd * ts:(hd + 1) * ts], wuv_ref[hd])

    @pl.when(is_last)
    def _():
        drain()


def _mla_sample(pt_flat, qc, ckv, kpe, cache_ckv, cache_kpe, wuv, n_prompt_rows, dec_b, ts, n_pages, n_grp):
    row0 = n_prompt_rows // ts
    newrow = lambda b, s, pt: (row0 + b, 0)

    grid_spec = pltpu.PrefetchScalarGridSpec(
        num_scalar_prefetch=1,
        grid=(dec_b, n_pages // n_grp),
        in_specs=[pl.BlockSpec((ts, C_HEADS * C_QK), newrow), pl.BlockSpec((ts, C_KV_LORA), newrow),
                  pl.BlockSpec((ts, C_ROPE_DIM), newrow), pl.BlockSpec(wuv.shape, lambda b, s, pt: (0, 0, 0)),
                  pl.BlockSpec(memory_space=pl.ANY), pl.BlockSpec(memory_space=pl.ANY)],
        out_specs=pl.BlockSpec((ts, C_WIDTH), lambda b, s, pt: (b, 0)),
        scratch_shapes=[pltpu.VMEM((2, n_grp, PAGE, C_KV_LORA), F32), pltpu.VMEM((2, n_grp, C_ROPE_DIM, PAGE), F32),
                        pltpu.SemaphoreType.DMA((2, 2)),
                        pltpu.VMEM((C_HEADS * ts, 1), F32), pltpu.VMEM((C_HEADS * ts, 1), F32),
                        pltpu.VMEM((C_HEADS * ts, C_KV_LORA), F32)],
    )
    return pl.pallas_call(
        functools.partial(_mla_sample_kernel, n_grp=n_grp, n_pages=n_pages),
        grid_spec=grid_spec,
        out_shape=jax.ShapeDtypeStruct((dec_b * ts, C_WIDTH), F32),
        compiler_params=_cparams(("arbitrary", "arbitrary"), 40),
        name="mla_sample",
    )(pt_flat, qc, ckv, kpe, wuv, cache_ckv, cache_kpe)


def _ln_silu(y, g, b):
    yc = y - jnp.mean(y, axis=-1, keepdims=True)
    yn = yc * lax.rsqrt(jnp.mean(yc * yc, axis=-1, keepdims=True) + LN_EPS) * g + b
    return yn * jax.nn.sigmoid(yn)


def _conv_prompt_kernel(u_ref, halo_ref, cw_ref, cb_ref, g_ref, b_ref, o_ref, ext_ref, y_ref):
    i = pl.program_id(1)
    tb = u_ref.shape[0]
    ext_ref[0:CONV_HALO, :] = jnp.where(i > 0, halo_ref[...], 0.0)
    ext_ref[CONV_HALO:, :] = u_ref[...]
    off = CONV_HALO - (CONV_W - 1)
    cw = 256
    for c0 in range(0, D_CONV, cw):
        acc = jnp.zeros((tb, cw), F32) + cb_ref[:, c0:c0 + cw]
        for k in range(CONV_W):
            acc = acc + ext_ref[off + k:off + k + tb, c0:c0 + cw] * cw_ref[k:k + 1, c0:c0 + cw]
        y_ref[:, c0:c0 + cw] = acc
    o_ref[...] = _ln_silu(y_ref[...], g_ref[...], b_ref[...])


def _conv_prompt(u, cw, cb, g, b, nb, tp):
    nt = nb * tp
    tb = QB
    nq = tp // tb
    hb = tb // CONV_HALO
    const = lambda bb, i: (0, 0)
    return pl.pallas_call(
        _conv_prompt_kernel,
        grid=(nb, nq),
        in_specs=[pl.BlockSpec((tb, D_CONV), lambda bb, i: (bb * nq + i, 0)),
                  pl.BlockSpec((CONV_HALO, D_CONV), lambda bb, i: (jnp.maximum((bb * nq + i) * hb - 1, 0), 0)),
                  pl.BlockSpec(cw.shape, const), pl.BlockSpec((1, D_CONV), const), pl.BlockSpec((1, D_CONV), const),
                  pl.BlockSpec((1, D_CONV), const)],
        out_specs=pl.BlockSpec((tb, D_CONV), lambda bb, i: (bb * nq + i, 0)),
        out_shape=jax.ShapeDtypeStruct((nt, D_CONV), F32),
        scratch_shapes=[pltpu.VMEM((CONV_HALO + tb, D_CONV), F32), pltpu.VMEM((tb, D_CONV), F32)],
        compiler_params=_cparams(("parallel", "parallel"), 32),
        name="conv_prompt",
    )(u, u, cw, cb, g, b)


def _conv_sample_kernel(ext_ref, cw_ref, cb_ref, g_ref, b_ref, o_ref):
    bb, n_ext, _ = ext_ref.shape
    ts = n_ext - (CONV_W - 1)
    acc = jnp.zeros((bb, ts, D_CONV), F32) + cb_ref[...]
    for k in range(CONV_W):
        acc = acc + ext_ref[:, k:k + ts, :] * cw_ref[k:k + 1, :]
    o_ref[...] = _ln_silu(acc.reshape(bb * ts, D_CONV), g_ref[...], b_ref[...])


def _conv_sample(ext, cw, cb, g, b, bb):
    dec_b, n_ext, _ = ext.shape
    ts = n_ext - (CONV_W - 1)
    const = lambda i: (0, 0)
    return pl.pallas_call(
        _conv_sample_kernel,
        grid=(dec_b // bb,),
        in_specs=[pl.BlockSpec((bb, n_ext, D_CONV), lambda i: (i, 0, 0)),
                  pl.BlockSpec(cw.shape, const), pl.BlockSpec((1, D_CONV), const), pl.BlockSpec((1, D_CONV), const),
                  pl.BlockSpec((1, D_CONV), const)],
        out_specs=pl.BlockSpec((bb * ts, D_CONV), lambda i: (i, 0)),
        out_shape=jax.ShapeDtypeStruct((dec_b * ts, D_CONV), F32),
        compiler_params=_cparams(("parallel",), 32),
        name="conv_sample",
    )(ext, cw, cb, g, b)


def _router_weights(router_group, router_group_bias, router_expert, router_expert_bias):
    pad = LANES - N_EXPERTS - N_GROUPS
    w = jnp.concatenate([router_expert, router_group, jnp.zeros((D_MODEL, pad), F32)], axis=1)
    b = jnp.concatenate([router_expert_bias, router_group_bias, jnp.zeros((pad,), F32)]).reshape(1, LANES)
    wh = w.astype(BF16)
    wl = (w - wh.astype(F32)).astype(BF16)
    return wh, wl, b


def _cumsum_matrix():
    j = jnp.arange(2 * QB)[:, None] % QB
    c = jnp.arange(2 * QB)[None, :]
    return jnp.where((c >= QB) | (j > c), 1.0, 0.0).astype(BF16)


def kernel(x_prompt, x_sample, cache_sb_k, cache_sb_v, state_pool, cache_mla_ckv, cache_mla_kpe, state_conv, page_table, meta_tokens, norm_mix, norm_ffn, norm_final, w_in_ab, w_pool, pool_scale, w_out_ab, w_in_cd, mla_q_norm, mla_kv_norm, w_uq, w_uk, w_uv, conv_w, conv_b, conv_norm_g, conv_norm_b, w_out_cd, router_group, router_group_bias, router_expert, router_expert_bias, expert_w_gate, expert_w_up, expert_w_down):
    nb, seq, _ = x_prompt.shape
    dec_b, ts, _ = x_sample.shape
    n_pages = page_table.shape[1]
    past_len = n_pages * PAGE
    t_real = seq + N_META
    tp = -(-t_real // QB) * QB
    n_p = nb * tp
    n_s = dec_b * ts
    nt = n_p + n_s
    tm = 256 if nt % 256 == 0 else 128
    assert ts == 8 and n_p % tm == 0 and n_s % tm == 0 and past_len >= POOL_MAX

    meta = jnp.broadcast_to(meta_tokens[None], (nb, N_META, D_MODEL))
    xp = jnp.concatenate([meta, x_prompt, jnp.zeros((nb, tp - t_real, D_MODEL), F32)], axis=1)
    x = jnp.concatenate([xp.reshape(n_p, D_MODEL), x_sample.reshape(n_s, D_MODEL)], axis=0)
    pt_flat = page_table.reshape(-1).astype(I32)
    uo = _cumsum_matrix()
    grp = next(g for g in (SB_PAGES_PER_STEP, 8, 1) if n_pages % g == 0)
    bb = 16 if dec_b % 16 == 0 else dec_b

    def prompt_rows(a, n_last):
        return a[:n_p].reshape(nb, tp, -1)[:, t_real - n_last:t_real]

    def moe_layer(layer, x_res, a_p, a_s, b_p, b_s, w_out):
        rwh, rwl, rb = _router_weights(router_group[layer], router_group_bias[layer], router_expert[layer],
                                       router_expert_bias[layer])
        x_res, hn, ri, rf, cnt = _out_proj(x_res, a_p, a_s, b_p, b_s, w_out.astype(BF16),
                                           norm_ffn[layer].reshape(1, -1), rwh, rwl, rb, tm)
        sched = _moe_schedule(ri, cnt, nt, tm)
        o = _moe(hn, *sched, expert_w_gate, expert_w_up, expert_w_down, layer)
        return x_res, o, rf

    q, k, v, u = _in_proj_ab(x, norm_mix[0].reshape(1, -1), w_in_ab.astype(BF16), tm)
    oa_p = _sb_prompt(q, k, v, uo, nb, tp)
    page_rows = (-1, PAGE * A_KV_HEADS, A_HEAD_DIM)
    oa_s = _sb_sample(pt_flat, q, k, v, cache_sb_k.reshape(page_rows), cache_sb_v.reshape(page_rows), uo, n_p, dec_b, ts,
                      n_pages, grp)
    wp = w_pool.astype(BF16)
    ps = pool_scale.reshape(1, -1)
    ob_p = _pool_prompt(u, wp, ps, nb, tp)
    pool_ext = jnp.concatenate([state_pool, u[n_p:].reshape(dec_b, ts, B_WIDTH)], axis=1)
    ob_s = _pool_sample(pool_ext, wp, ps, bb)
    x, o, rf = moe_layer(0, x, oa_p, oa_s, ob_p, ob_s, w_out_ab)

    o1 = C_Q_LORA
    o2 = o1 + C_KV_LORA
    o3 = o2 + C_ROPE_DIM
    w_cd = jnp.concatenate([w_in_cd[:, :o2], w_in_cd[:, o3:], w_in_cd[:, o2:o3],
                            jnp.zeros((D_MODEL, LANES - C_ROPE_DIM), F32)], axis=1).astype(BF16)
    wq = w_uq.reshape(C_Q_LORA, C_HEADS, C_NOPE_DIM + C_ROPE_DIM)
    wq_pe = jnp.concatenate([wq[:, :, C_NOPE_DIM:], jnp.zeros((C_Q_LORA, C_HEADS, LANES - C_ROPE_DIM), F32)], axis=2)
    wuq = jnp.concatenate([wq[:, :, :C_NOPE_DIM].reshape(C_Q_LORA, -1), wq_pe.reshape(C_Q_LORA, -1)], axis=1).astype(BF16)
    wuk = jnp.transpose(w_uk, (1, 2, 0)).astype(BF16)
    wuv = jnp.transpose(w_uv, (1, 0, 2)).astype(BF16)
    half = C_ROPE_DIM // 2
    inv_freq = ROPE_THETA ** (-jnp.arange(half, dtype=F32) / half)
    pos = jnp.concatenate([jnp.tile(jnp.arange(tp), nb), jnp.tile(past_len + jnp.arange(ts), dec_b)]).astype(F32)
    ang = pos[:, None] * inv_freq[None, :]
    zpad = jnp.zeros((nt, LANES - C_ROPE_DIM), F32)
    cos = jnp.concatenate([jnp.cos(ang), jnp.cos(ang), zpad], axis=1)
    sin = jnp.concatenate([jnp.sin(ang), jnp.sin(ang), zpad], axis=1)
    x, qc, ckv, kpe, kc, uc = _in_proj_cd(x, o, o, rf, norm_mix[1].reshape(1, -1), w_cd, mla_q_norm.reshape(1, -1),
                                          mla_kv_norm.reshape(1, -1), wuq, wuk, cos, sin, tm)
    oc_p = _mla_prompt(qc, kc, wuv, nb, tp)
    grp_c = next(g for g in (MLA_PAGES_PER_STEP, 16, 8, 1) if n_pages % g == 0)
    oc_s = _mla_sample(pt_flat, qc, ckv, kpe, cache_mla_ckv, jnp.swapaxes(cache_mla_kpe, 1, 2), wuv, n_p, dec_b, ts,
                       n_pages, grp_c)
    conv_args = (conv_w, conv_b.reshape(1, -1), conv_norm_g.reshape(1, -1), conv_norm_b.reshape(1, -1))
    od_p = _conv_prompt(uc, *conv_args, nb, tp)
    conv_ext = jnp.concatenate([state_conv, uc[n_p:].reshape(dec_b, ts, D_CONV)], axis=1)
    od_s = _conv_sample(conv_ext, *conv_args, bb)
    x, o, rf = moe_layer(1, x, oc_p, oc_s, od_p, od_s, w_out_cd)

    y = _final_norm(x, o, rf, norm_final.reshape(1, -1), tm)

    y_prompt = y[:n_p].reshape(nb, tp, D_MODEL)[:, N_META:t_real]
    y_sample = y[n_p:].reshape(dec_b, ts, D_MODEL)
    kv4 = (A_KV_HEADS, A_HEAD_DIM)
    return (y_prompt, y_sample,
            prompt_rows(k, t_real).reshape(nb, t_real, *kv4), prompt_rows(v, t_real).reshape(nb, t_real, *kv4),
            k[n_p:].reshape(dec_b, ts, *kv4), v[n_p:].reshape(dec_b, ts, *kv4),
            prompt_rows(u, POOL_MAX - 1), pool_ext[:, ts:],
            prompt_rows(ckv, t_real), prompt_rows(kpe, t_real),
            ckv[n_p:].reshape(dec_b, ts, C_KV_LORA), kpe[n_p:].reshape(dec_b, ts, C_ROPE_DIM),
            prompt_rows(uc, CONV_W - 1), conv_ext[:, ts:])
```

```python
import functools

import jax
import jax.numpy as jnp
from jax import lax
from jax.experimental import pallas as pl
from jax.experimental.pallas import tpu as pltpu

F32 = jnp.float32
BF16 = jnp.bfloat16
I32 = jnp.int32

D_MODEL = 2048
N_META = 16
RMS_EPS = 1e-6
LN_EPS = 1e-5
NEG_INF = -1e30
PAGE = 128
LANES = 128
QB = 128

A_HEADS = 8
A_KV_HEADS = 4
A_HEAD_DIM = 128
A_WIDTH = A_HEADS * A_HEAD_DIM
A_KV_WIDTH = A_KV_HEADS * A_HEAD_DIM
A_SCALE = A_HEAD_DIM ** -0.5

POOL_WINDOWS = (2, 4, 8, 16)
POOL_MAX = 16
B_WIDTH = 1024
B_GROUP_WIDTH = B_WIDTH // len(POOL_WINDOWS)

C_HEADS = 8
C_NOPE_DIM = 128
C_ROPE_DIM = 64
C_V_DIM = 128
C_Q_LORA = 512
C_KV_LORA = 256
C_WIDTH = C_HEADS * C_V_DIM
C_SCALE = (C_NOPE_DIM + C_ROPE_DIM) ** -0.5
ROPE_THETA = 10000.0
C_QK = C_KV_LORA + LANES

D_CONV = 1024
CONV_W = 31
CONV_HALO = 32

N_GROUPS = 4
EXPERTS_PER_GROUP = 8
N_EXPERTS = 32
D_EXPERT = 512
SB_PAGES_PER_STEP = 16
MLA_PAGES_PER_STEP = 64
MOE_TILE = 256
MOE_DMA_UNROLL = 8


def _cparams(sem, vmem_mb):
    return pltpu.CompilerParams(dimension_semantics=sem, vmem_limit_bytes=vmem_mb << 20)


def _rms(x, g):
    return x * lax.rsqrt(jnp.mean(x * x, axis=-1, keepdims=True) + RMS_EPS) * g


def _dot(a, b):
    return jnp.dot(a, b, preferred_element_type=F32)


def _dot_nt(a, b):
    return lax.dot_general(a, b, (((1,), (1,)), ((), ())), preferred_element_type=F32)


def _softplus(z):
    return jnp.maximum(z, 0.0) + jnp.log(1.0 + jnp.exp(-jnp.abs(z)))


def _moe_combine(x_ref, o1_ref, o2_ref, rf_ref):
    rf = rf_ref[...]
    return x_ref[...] + rf[:, 0:1] * o1_ref[...] + rf[:, 1:2] * o2_ref[...]


def _in_ab_kernel(x_ref, g_ref, w_ref, q_ref, k_ref, v_ref, u_ref):
    h = _rms(x_ref[...], g_ref[...]).astype(BF16)
    q_ref[...] = _dot(h, w_ref[:, 0:A_WIDTH]) * A_SCALE
    k_ref[...] = _dot(h, w_ref[:, A_WIDTH:A_WIDTH + A_KV_WIDTH])
    v_ref[...] = _dot(h, w_ref[:, A_WIDTH + A_KV_WIDTH:A_WIDTH + 2 * A_KV_WIDTH])
    u_ref[...] = _dot(h, w_ref[:, A_WIDTH + 2 * A_KV_WIDTH:])


def _in_proj_ab(x, g, w, tm):
    nt = x.shape[0]
    n_in = w.shape[1]
    row = lambda i: (i, 0)
    const = lambda i: (0, 0)
    return pl.pallas_call(
        _in_ab_kernel,
        grid=(nt // tm,),
        in_specs=[pl.BlockSpec((tm, D_MODEL), row), pl.BlockSpec((1, D_MODEL), const),
                  pl.BlockSpec((D_MODEL, n_in), const)],
        out_specs=[pl.BlockSpec((tm, A_WIDTH), row), pl.BlockSpec((tm, A_KV_WIDTH), row),
                   pl.BlockSpec((tm, A_KV_WIDTH), row), pl.BlockSpec((tm, B_WIDTH), row)],
        out_shape=[jax.ShapeDtypeStruct((nt, A_WIDTH), F32), jax.ShapeDtypeStruct((nt, A_KV_WIDTH), F32),
                   jax.ShapeDtypeStruct((nt, A_KV_WIDTH), F32), jax.ShapeDtypeStruct((nt, B_WIDTH), F32)],
        compiler_params=_cparams(("parallel",), 48),
        name="in_proj_ab",
    )(x, g, w)


def _rot_half(x, lane_period):
    n = x.shape[-1]
    half = C_ROPE_DIM // 2
    lane = lax.broadcasted_iota(I32, x.shape, x.ndim - 1) % lane_period
    fwd = pltpu.roll(x, n - half, x.ndim - 1)
    bwd = pltpu.roll(x, half, x.ndim - 1)
    return jnp.where(lane < half, -fwd, jnp.where(lane < C_ROPE_DIM, bwd, 0.0))


def _in_cd_kernel(x_ref, o1_ref, o2_ref, rf_ref, g_ref, w_ref, qn_ref, kvn_ref, wuq_ref, wuk_ref, cos_ref, sin_ref,
                  xo_ref, qc_ref, ckv_ref, kpe_ref, kc_ref, u_ref):
    x = _moe_combine(x_ref, o1_ref, o2_ref, rf_ref)
    xo_ref[...] = x
    h = _rms(x, g_ref[...]).astype(BF16)
    cos = cos_ref[...]
    sin = sin_ref[...]
    cq = _rms(_dot(h, w_ref[:, 0:C_Q_LORA]), qn_ref[...]).astype(BF16)
    for hd in range(C_HEADS):
        qn = _dot(cq, wuq_ref[:, hd * LANES:(hd + 1) * LANES]).astype(BF16)
        qc_ref[:, hd * C_QK:hd * C_QK + C_KV_LORA] = _dot(qn, wuk_ref[hd]) * C_SCALE
        qp = _dot(cq, wuq_ref[:, (C_HEADS + hd) * LANES:(C_HEADS + hd + 1) * LANES])
        qc_ref[:, hd * C_QK + C_KV_LORA:(hd + 1) * C_QK] = (qp * cos + _rot_half(qp, LANES) * sin) * C_SCALE
    o1 = C_Q_LORA
    o2 = o1 + C_KV_LORA
    ckv = _rms(_dot(h, w_ref[:, o1:o2]), kvn_ref[...])
    ckv_ref[...] = ckv
    o3 = o2 + 2 * D_CONV
    kr = _dot(h, w_ref[:, o3:o3 + LANES])
    kpe = kr * cos + _rot_half(kr, LANES) * sin
    kpe_ref[...] = kpe[:, 0:C_ROPE_DIM]
    kc_ref[:, 0:C_KV_LORA] = ckv.astype(BF16)
    kc_ref[:, C_KV_LORA:] = kpe.astype(BF16)
    ga = _dot(h, w_ref[:, o2:o2 + D_CONV])
    gb = _dot(h, w_ref[:, o2 + D_CONV:o3])
    u_ref[...] = ga * jax.nn.sigmoid(gb)


def _in_proj_cd(x, o1, o2, rf, g, w, qn, kvn, wuq, wuk, cos, sin, tm):
    nt = x.shape[0]
    nb = nt // tm
    row = lambda i: (i, 0)
    row2 = lambda i: (i + nb, 0)
    const = lambda i: (0, 0)
    return pl.pallas_call(
        _in_cd_kernel,
        grid=(nb,),
        in_specs=[pl.BlockSpec((tm, D_MODEL), row), pl.BlockSpec((tm, D_MODEL), row), pl.BlockSpec((tm, D_MODEL), row2),
                  pl.BlockSpec((tm, LANES), row), pl.BlockSpec((1, D_MODEL), const),
                  pl.BlockSpec(w.shape, const), pl.BlockSpec((1, C_Q_LORA), const), pl.BlockSpec((1, C_KV_LORA), const),
                  pl.BlockSpec(wuq.shape, const), pl.BlockSpec(wuk.shape, lambda i: (0, 0, 0)),
                  pl.BlockSpec((tm, LANES), row), pl.BlockSpec((tm, LANES), row)],
        out_specs=[pl.BlockSpec((tm, D_MODEL), row), pl.BlockSpec((tm, C_HEADS * C_QK), row),
                   pl.BlockSpec((tm, C_KV_LORA), row), pl.BlockSpec((tm, C_ROPE_DIM), row),
                   pl.BlockSpec((tm, C_QK), row), pl.BlockSpec((tm, D_CONV), row)],
        out_shape=[jax.ShapeDtypeStruct((nt, D_MODEL), F32), jax.ShapeDtypeStruct((nt, C_HEADS * C_QK), F32),
                   jax.ShapeDtypeStruct((nt, C_KV_LORA), F32), jax.ShapeDtypeStruct((nt, C_ROPE_DIM), F32),
                   jax.ShapeDtypeStruct((nt, C_QK), BF16), jax.ShapeDtypeStruct((nt, D_CONV), F32)],
        compiler_params=_cparams(("parallel",), 56),
        name="in_proj_cd",
    )(x, o1, o2, rf, g, w, qn, kvn, wuq, wuk, cos, sin)


def _out_proj_kernel(x_ref, ap_ref, as_ref, bp_ref, bs_ref, w_ref, g_ref, rwh_ref, rwl_ref, rb_ref,
                     xo_ref, hn_ref, ri_ref, rf_ref, cnt_ref, carry_ref, *, n_prompt_tiles):
    tm = x_ref.shape[0]
    step = pl.program_id(0)

    @pl.when(step == 0)
    def _():
        carry_ref[...] = jnp.zeros_like(carry_ref)

    is_prompt = step < n_prompt_tiles
    a = jnp.where(is_prompt, ap_ref[...], as_ref[...])
    b = jnp.where(is_prompt, bp_ref[...], bs_ref[...])
    ab = jnp.concatenate([a, b], axis=1).astype(BF16)
    x = x_ref[...] + _dot(ab, w_ref[...])
    xo_ref[...] = x
    hn = _rms(x, g_ref[...])
    hn_ref[...] = hn
    hh = hn.astype(BF16)
    hl = (hn - hh.astype(F32)).astype(BF16)
    logit = _dot(hh, rwh_ref[...]) + _dot(hl, rwh_ref[...]) + _dot(hh, rwl_ref[...]) + rb_ref[...]
    lane = lax.broadcasted_iota(I32, (tm, LANES), 1)
    lanef = lane.astype(F32)
    big = 1e9
    is_g = (lane >= N_EXPERTS) & (lane < N_EXPERTS + N_GROUPS)
    gl = jnp.where(is_g, logit, NEG_INF)
    gmax = jnp.max(gl, axis=1, keepdims=True)
    grp_lane = jnp.min(jnp.where(gl == gmax, lanef, big), axis=1, keepdims=True)
    g_w = 1.0 / jnp.sum(jnp.where(is_g, jnp.exp(gl - gmax), 0.0), axis=1, keepdims=True)
    grp = grp_lane.astype(I32) - N_EXPERTS
    in_grp = (lane < N_EXPERTS) & (jnp.right_shift(lane, 3) == grp)
    el = jnp.where(in_grp, logit, NEG_INF)
    v1 = jnp.max(el, axis=1, keepdims=True)
    i1 = jnp.min(jnp.where(el == v1, lanef, big), axis=1, keepdims=True)
    el2 = jnp.where(lanef == i1, NEG_INF, el)
    v2 = jnp.max(el2, axis=1, keepdims=True)
    i2 = jnp.min(jnp.where(el2 == v2, lanef, big), axis=1, keepdims=True)
    e21 = jnp.exp(v2 - v1)
    gate1 = g_w / (1.0 + e21)
    gate2 = gate1 * e21
    oh1 = lanef == i1
    oh2 = lanef == i2
    cf = jnp.where(oh1 | oh2, 1.0, 0.0)
    r_i = lax.broadcasted_iota(I32, (tm, tm), 0)
    c_i = lax.broadcasted_iota(I32, (tm, tm), 1)
    tri = jnp.where(r_i > c_i, 1.0, 0.0).astype(BF16)
    carry = carry_ref[...]
    cum = _dot(tri, cf.astype(BF16)) + carry[0:1, :]
    rank1 = jnp.sum(jnp.where(oh1, cum, 0.0), axis=1, keepdims=True)
    rank2 = jnp.sum(jnp.where(oh2, cum, 0.0), axis=1, keepdims=True)
    carry = carry + jnp.sum(cf, axis=0, keepdims=True)
    carry_ref[...] = carry
    cnt_ref[...] = carry
    ri = jnp.where(lane == 0, i1, jnp.where(lane == 1, i2, jnp.where(lane == 2, rank1, jnp.where(lane == 3, rank2, 0.0))))
    ri_ref[...] = ri.astype(I32)
    rf_ref[...] = jnp.where(lane == 0, gate1, jnp.where(lane == 1, gate2, 0.0))


def _out_proj(x, a_p, a_s, b_p, b_s, w, g, rwh, rwl, rb, tm):
    nt = x.shape[0]
    nbp = a_p.shape[0] // tm
    row = lambda i: (i, 0)
    prow = lambda i: (jnp.minimum(i, nbp - 1), 0)
    srow = lambda i: (jnp.maximum(i - nbp, 0), 0)
    const = lambda i: (0, 0)
    return pl.pallas_call(
        functools.partial(_out_proj_kernel, n_prompt_tiles=nbp),
        grid=(nt // tm,),
        in_specs=[pl.BlockSpec((tm, D_MODEL), row), pl.BlockSpec((tm, a_p.shape[1]), prow),
                  pl.BlockSpec((tm, a_s.shape[1]), srow), pl.BlockSpec((tm, b_p.shape[1]), prow),
                  pl.BlockSpec((tm, b_s.shape[1]), srow),
                  pl.BlockSpec(w.shape, const), pl.BlockSpec((1, D_MODEL), const),
                  pl.BlockSpec((D_MODEL, LANES), const), pl.BlockSpec((D_MODEL, LANES), const), pl.BlockSpec((1, LANES), const)],
        out_specs=[pl.BlockSpec((tm, D_MODEL), row), pl.BlockSpec((tm, D_MODEL), row),
                   pl.BlockSpec((tm, LANES), row), pl.BlockSpec((tm, LANES), row), pl.BlockSpec((8, LANES), const)],
        out_shape=[jax.ShapeDtypeStruct((nt, D_MODEL), F32), jax.ShapeDtypeStruct((nt, D_MODEL), F32),
                   jax.ShapeDtypeStruct((nt, LANES), I32), jax.ShapeDtypeStruct((nt, LANES), F32),
                   jax.ShapeDtypeStruct((8, LANES), F32)],
        scratch_shapes=[pltpu.VMEM((8, LANES), F32)],
        compiler_params=_cparams(("arbitrary",), 48),
        name="out_proj_router",
    )(x, a_p, a_s, b_p, b_s, w, g, rwh, rwl, rb)


def _final_norm_kernel(x_ref, o1_ref, o2_ref, rf_ref, g_ref, y_ref):
    y_ref[...] = _rms(_moe_combine(x_ref, o1_ref, o2_ref, rf_ref), g_ref[...])


def _final_norm(x, o, rf, g, tm):
    nt = x.shape[0]
    nb = nt // tm
    row = lambda i: (i, 0)
    return pl.pallas_call(
        _final_norm_kernel,
        grid=(nb,),
        in_specs=[pl.BlockSpec((tm, D_MODEL), row), pl.BlockSpec((tm, D_MODEL), row),
                  pl.BlockSpec((tm, D_MODEL), lambda i: (i + nb, 0)), pl.BlockSpec((tm, LANES), row),
                  pl.BlockSpec((1, D_MODEL), lambda i: (0, 0))],
        out_specs=pl.BlockSpec((tm, D_MODEL), row),
        out_shape=jax.ShapeDtypeStruct((nt, D_MODEL), F32),
        compiler_params=_cparams(("parallel",), 32),
        name="final_norm",
    )(x, o, o, rf, g)


def _moe_kernel(te_ref, nu_ref, tv_ref, did_ref, hn_ref, wg_ref, wu_ref, wd_ref, o_ref,
                xbuf, ybuf, gsem, ssem):
    i = pl.program_id(0)
    n_used = nu_ref[0]
    tmm = ybuf.shape[0]
    n_tok = hn_ref.shape[0]

    def gather(tile, slot):
        base = tile * tmm

        def body(c8, c):
            for u in range(MOE_DMA_UNROLL):
                r = c8 * MOE_DMA_UNROLL + u
                did = did_ref[base + r]
                rid = jnp.where(did >= n_tok, did - n_tok, did)
                pltpu.make_async_copy(hn_ref.at[pl.ds(rid, 1), :], xbuf.at[slot, pl.ds(r, 1), :], gsem.at[slot]).start()
            return c

        lax.fori_loop(0, tmm // MOE_DMA_UNROLL, body, 0)

    def gather_wait(slot):
        pltpu.make_async_copy(hn_ref.at[pl.ds(0, tmm), :], xbuf.at[slot], gsem.at[slot]).wait()

    def scatter_wait(tile):
        n_valid = tv_ref[tile]

        @pl.when(n_valid == tmm)
        def _():
            pltpu.make_async_copy(ybuf, o_ref.at[pl.ds(0, tmm), :], ssem.at[0]).wait()

        @pl.when(n_valid < tmm)
        def _():
            def body(r, c):
                pltpu.make_async_copy(ybuf.at[pl.ds(0, 1), :], o_ref.at[pl.ds(0, 1), :], ssem.at[0]).wait()
                return c

            lax.fori_loop(0, n_valid, body, 0)

    @pl.when(i == 0)
    def _():
        gather(0, 0)

    @pl.when(i < n_used)
    def _():
        slot = i % 2

        @pl.when(i + 1 < n_used)
        def _():
            gather(i + 1, 1 - slot)

        gather_wait(slot)
        x = xbuf[slot].astype(BF16)
        hid = jax.nn.silu(_dot(x, wg_ref[...].astype(BF16))) * _dot(x, wu_ref[...].astype(BF16))
        y = _dot(hid.astype(BF16), wd_ref[...].astype(BF16))

        @pl.when(i > 0)
        def _():
            scatter_wait(i - 1)

        ybuf[...] = y
        base = i * tmm

        def scatter_row(r, priority=0):
            did = did_ref[base + r]
            pltpu.make_async_copy(ybuf.at[pl.ds(r, 1), :], o_ref.at[pl.ds(did, 1), :], ssem.at[0]).start(priority=priority)

        def body_unrolled(c8, c):
            for u in range(MOE_DMA_UNROLL):
                scatter_row(c8 * MOE_DMA_UNROLL + u, priority=u % 2)
            return c

        def body_single(r, c):
            scatter_row(r)
            return c

        n_valid = tv_ref[i]
        n_chunks = n_valid // MOE_DMA_UNROLL
        lax.fori_loop(0, n_chunks, body_unrolled, 0)
        lax.fori_loop(n_chunks * MOE_DMA_UNROLL, n_valid, body_single, 0)

        @pl.when(i == n_used - 1)
        def _():
            scatter_wait(i)


def _moe(hn, tile_expert, n_used, tile_valid, dst_ids, w_gate, w_up, w_down, layer):
    nt = hn.shape[0]
    n_tiles = tile_expert.shape[0]
    wspec = lambda shp: pl.BlockSpec((None, None) + shp, lambda i, te, nu, tv, did: (layer, te[i], 0, 0))
    grid_spec = pltpu.PrefetchScalarGridSpec(
        num_scalar_prefetch=4,
        grid=(n_tiles,),
        in_specs=[pl.BlockSpec(memory_space=pl.ANY), wspec((D_MODEL, D_EXPERT)), wspec((D_MODEL, D_EXPERT)),
                  wspec((D_EXPERT, D_MODEL))],
        out_specs=pl.BlockSpec(memory_space=pl.ANY),
        scratch_shapes=[pltpu.VMEM((2, MOE_TILE, D_MODEL), F32), pltpu.VMEM((MOE_TILE, D_MODEL), F32),
                        pltpu.SemaphoreType.DMA((2,)), pltpu.SemaphoreType.DMA((1,))],
    )
    return pl.pallas_call(
        _moe_kernel,
        grid_spec=grid_spec,
        out_shape=jax.ShapeDtypeStruct((2 * nt, D_MODEL), F32),
        compiler_params=_cparams(("arbitrary",), 56),
        name="moe_experts",
    )(tile_expert, n_used, tile_valid, dst_ids, hn, w_gate, w_up, w_down)


def _moe_pos_kernel(ri_ref, offs_ref, pos_ref):
    ri = ri_ref[...]
    lane = lax.broadcasted_iota(I32, ri.shape, 1)
    offs = offs_ref[...]
    pos1 = jnp.sum(jnp.where(lane == ri[:, 0:1], offs, 0.0), axis=1, keepdims=True).astype(I32) + ri[:, 2:3]
    pos2 = jnp.sum(jnp.where(lane == ri[:, 1:2], offs, 0.0), axis=1, keepdims=True).astype(I32) + ri[:, 3:4]
    pos_ref[...] = jnp.where(lane == 0, pos1, jnp.where(lane == 1, pos2, 0))


def _moe_positions(ri, offs, tm):
    nt = ri.shape[0]
    offs_row = jnp.zeros((1, LANES), F32).at[0, :N_EXPERTS].set(offs.astype(F32))
    return pl.pallas_call(
        _moe_pos_kernel,
        grid=(nt // tm,),
        in_specs=[pl.BlockSpec((tm, LANES), lambda i: (i, 0)), pl.BlockSpec((1, LANES), lambda i: (0, 0))],
        out_specs=pl.BlockSpec((tm, LANES), lambda i: (i, 0)),
        out_shape=jax.ShapeDtypeStruct((nt, LANES), I32),
        compiler_params=_cparams(("parallel",), 32),
        name="moe_positions",
    )(ri, offs_row)


def _moe_schedule(ri, cnt, nt, tm):
    counts = cnt[0, :N_EXPERTS].astype(I32)
    padded = ((counts + MOE_TILE - 1) // MOE_TILE) * MOE_TILE
    ends = jnp.cumsum(padded)
    offs = ends - padded
    n_tiles = -(-2 * nt // MOE_TILE) + N_EXPERTS
    n_used = (ends[-1] // MOE_TILE).astype(I32)
    tile_raw = jnp.arange(n_tiles, dtype=I32)
    tile = jnp.minimum(tile_raw, n_used - 1)
    tile_expert = jnp.minimum(jnp.sum((ends // MOE_TILE)[None, :] <= tile[:, None], axis=1), N_EXPERTS - 1).astype(I32)
    rows_left = counts[tile_expert] - (tile * MOE_TILE - offs[tile_expert])
    tile_valid = jnp.where(tile_raw < n_used, jnp.clip(rows_left, 0, MOE_TILE), 0).astype(I32)
    pos = _moe_positions(ri, offs, tm)
    n_sorted = n_tiles * MOE_TILE
    dst_ids = jnp.zeros((n_sorted,), I32).at[jnp.concatenate([pos[:, 0], pos[:, 1]])].set(
        jnp.arange(2 * nt, dtype=I32), unique_indices=True)
    return tile_expert, n_used.reshape(1), tile_valid, dst_ids


def _sb_blocks(q, kt, vt, uo, acc, mask):
    n_h = len(q)
    rows = q[0].shape[0]
    n_blk = kt[0].shape[0] // QB
    z = jnp.concatenate([_dot_nt(q[h], kt[h]) for h in range(n_h)], axis=0)
    sp = _softplus(z)
    lk = -sp
    if mask is not None:
        lk = jnp.where(mask, lk, 0.0)
    hi = lk.astype(BF16)
    lo = (lk - hi.astype(F32)).astype(BF16)
    split = jnp.concatenate([jnp.concatenate([hi[:, g * QB:(g + 1) * QB], lo[:, g * QB:(g + 1) * QB]], axis=1)
                             for g in range(n_blk)], axis=0)
    cs = _dot(split, uo)
    later = []
    for g in range(n_blk):
        cs_g = cs[g * n_h * rows:(g + 1) * n_h * rows]
        later.append(cs_g[:, 0:QB] + acc)
        acc = acc + cs_g[:, QB:]
    w = jnp.exp(z - sp + jnp.concatenate(later, axis=1))
    if mask is not None:
        w = jnp.where(mask, w, 0.0)
    w = w.astype(BF16)
    out = jnp.concatenate([_dot(w[h * rows:(h + 1) * rows], vt[h]) for h in range(n_h)], axis=0)
    return out, acc


def _sb_prompt_kernel(q_ref, k_ref, v_ref, uo_ref, o_ref, acc_ref, out_ref):
    i = pl.program_id(1)
    group = A_HEADS // A_KV_HEADS
    gw = group * A_HEAD_DIM
    q = [jnp.concatenate([q_ref[:, h * gw + g * A_HEAD_DIM:h * gw + (g + 1) * A_HEAD_DIM] for g in range(group)],
                         axis=0).astype(BF16) for h in range(A_KV_HEADS)]
    uo = uo_ref[...]
    t_i = lax.broadcasted_iota(I32, (A_HEADS * QB, QB), 0) % QB
    s_i = lax.broadcasted_iota(I32, (A_HEADS * QB, QB), 1)

    def step(j, mask):
        start = pl.multiple_of(j * QB, QB)
        kt = [k_ref[pl.ds(start, QB), h * A_HEAD_DIM:(h + 1) * A_HEAD_DIM].astype(BF16) for h in range(A_KV_HEADS)]
        vt = [v_ref[pl.ds(start, QB), h * A_HEAD_DIM:(h + 1) * A_HEAD_DIM].astype(BF16) for h in range(A_KV_HEADS)]
        return _sb_blocks(q, kt, vt, uo, acc_ref[...], mask)

    acc_ref[...] = jnp.zeros_like(acc_ref)
    out, acc = step(i, s_i < t_i)
    out_ref[...] = out
    acc_ref[...] = acc

    def body(jj, c):
        out, acc = step(i - 1 - jj, None)
        out_ref[...] += out
        acc_ref[...] = acc
        return c

    lax.fori_loop(0, i, body, 0)
    for hh in range(A_HEADS):
        o_ref[:, hh * A_HEAD_DIM:(hh + 1) * A_HEAD_DIM] = out_ref[hh * QB:(hh + 1) * QB, :]


def _sb_prompt(q, k, v, uo, nb, tp):
    nt = nb * tp
    nq = tp // QB
    return pl.pallas_call(
        _sb_prompt_kernel,
        grid=(nb, nq),
        in_specs=[pl.BlockSpec((QB, A_WIDTH), lambda b, i: (b * nq + i, 0)),
                  pl.BlockSpec((tp, A_KV_WIDTH), lambda b, i: (b, 0)),
                  pl.BlockSpec((tp, A_KV_WIDTH), lambda b, i: (b, 0)),
                  pl.BlockSpec((2 * QB, 2 * QB), lambda b, i: (0, 0))],
        out_specs=pl.BlockSpec((QB, A_WIDTH), lambda b, i: (b * nq + i, 0)),
        out_shape=jax.ShapeDtypeStruct((nt, A_WIDTH), F32),
        scratch_shapes=[pltpu.VMEM((A_HEADS * QB, QB), F32), pltpu.VMEM((A_HEADS * QB, QB), F32)],
        compiler_params=_cparams(("parallel", "arbitrary"), 40),
        name="sb_prompt",
    )(q, k, v, uo)


def _paged_prefetch(pt_ref, caches, bufs, sem, n_pages, n_grp, newest_first):
    n_steps = pl.num_programs(1)
    t = pl.program_id(0) * n_steps + pl.program_id(1)
    last = pl.num_programs(0) * n_steps - 1
    slot = t % 2

    def start(step, sl):
        seq = step // n_steps
        first = (step % n_steps) * n_grp
        for g in range(n_grp):
            idx = (n_pages - 1 - (first + g)) if newest_first else (first + g)
            page = pt_ref[seq * n_pages + idx]
            for k, (cache, buf) in enumerate(zip(caches, bufs)):
                pltpu.make_async_copy(cache.at[page], buf.at[sl, g], sem.at[k, sl]).start()

    def wait(sl):
        for k, (cache, buf) in enumerate(zip(caches, bufs)):
            pltpu.make_async_copy(cache.at[pl.ds(0, n_grp)], buf.at[sl], sem.at[k, sl]).wait()

    @pl.when(t == 0)
    def _():
        start(0, 0)

    start(jnp.minimum(t + 1, last), 1 - slot)
    wait(slot)
    return slot, t == last, lambda: wait(1 - slot)


def _sb_sample_kernel(pt_ref, q_ref, kn_ref, vn_ref, uo_ref, ck_ref, cv_ref, o_ref, kbuf, vbuf, sem, acc_ref, out_ref,
                      *, n_grp, n_pages):
    slot, is_last, drain = _paged_prefetch(pt_ref, (ck_ref, cv_ref), (kbuf, vbuf), sem, n_pages, n_grp, True)
    s = pl.program_id(1)
    ts = q_ref.shape[0]
    group = A_HEADS // A_KV_HEADS
    gw = group * A_HEAD_DIM
    qf = q_ref[...]
    q = [jnp.concatenate([qf[:, h * gw + g * A_HEAD_DIM:h * gw + (g + 1) * A_HEAD_DIM] for g in range(group)],
                         axis=0).astype(BF16) for h in range(A_KV_HEADS)]
    uo = uo_ref[...]
    n_rows = A_HEADS * ts

    @pl.when(s == 0)
    def _():
        pad = jnp.zeros((QB - ts, A_KV_WIDTH), F32)
        kn = jnp.concatenate([kn_ref[...], pad], axis=0)
        vn = jnp.concatenate([vn_ref[...], pad], axis=0)
        heads = lambda x: [x[:, h * A_HEAD_DIM:(h + 1) * A_HEAD_DIM].astype(BF16) for h in range(A_KV_HEADS)]
        t_i = lax.broadcasted_iota(I32, (n_rows, QB), 0) % ts
        s_i = lax.broadcasted_iota(I32, (n_rows, QB), 1)
        out, acc = _sb_blocks(q, heads(kn), heads(vn), uo, jnp.zeros((n_rows, QB), F32), s_i < t_i)
        out_ref[...] = out
        acc_ref[...] = acc

    def head_rows(buf, g, h):
        return buf[slot, g, pl.ds(h, PAGE, stride=A_KV_HEADS), :]

    kt = [jnp.concatenate([head_rows(kbuf, g, h) for g in range(n_grp)], axis=0).astype(BF16)
          for h in range(A_KV_HEADS)]
    vt = [jnp.concatenate([head_rows(vbuf, g, h) for g in range(n_grp)], axis=0).astype(BF16)
          for h in range(A_KV_HEADS)]
    out, acc = _sb_blocks(q, kt, vt, uo, acc_ref[...], None)
    out = out_ref[...] + out
    out_ref[...] = out
    acc_ref[...] = acc

    @pl.when(s == pl.num_programs(1) - 1)
    def _():
        for h in range(A_HEADS):
            o_ref[:, h * A_HEAD_DIM:(h + 1) * A_HEAD_DIM] = out[h * ts:(h + 1) * ts, :]

    @pl.when(is_last)
    def _():
        drain()


def _sb_sample(pt_flat, q, k, v, cache_k, cache_v, uo, n_prompt_rows, dec_b, ts, n_pages, n_grp):
    row0 = n_prompt_rows // ts
    newrow = lambda b, s, pt: (row0 + b, 0)

    page_shape = (PAGE * A_KV_HEADS, A_HEAD_DIM)
    grid_spec = pltpu.PrefetchScalarGridSpec(
        num_scalar_prefetch=1,
        grid=(dec_b, n_pages // n_grp),
        in_specs=[pl.BlockSpec((ts, A_WIDTH), newrow), pl.BlockSpec((ts, A_KV_WIDTH), newrow),
                  pl.BlockSpec((ts, A_KV_WIDTH), newrow), pl.BlockSpec((2 * QB, 2 * QB), lambda b, s, pt: (0, 0)),
                  pl.BlockSpec(memory_space=pl.ANY), pl.BlockSpec(memory_space=pl.ANY)],
        out_specs=pl.BlockSpec((ts, A_WIDTH), lambda b, s, pt: (b, 0)),
        scratch_shapes=[pltpu.VMEM((2, n_grp) + page_shape, F32), pltpu.VMEM((2, n_grp) + page_shape, F32),
                        pltpu.SemaphoreType.DMA((2, 2)),
                        pltpu.VMEM((A_HEADS * ts, QB), F32), pltpu.VMEM((A_HEADS * ts, QB), F32)],
    )
    return pl.pallas_call(
        functools.partial(_sb_sample_kernel, n_grp=n_grp, n_pages=n_pages),
        grid_spec=grid_spec,
        out_shape=jax.ShapeDtypeStruct((dec_b * ts, A_WIDTH), F32),
        compiler_params=_cparams(("arbitrary", "arbitrary"), 40),
        name="sb_sample",
    )(pt_flat, q, k, v, uo, cache_k, cache_v)


def _pool_project(d, gi, wp_ref, ps_ref):
    c0 = gi * B_GROUP_WIDTH
    return _dot(d.astype(BF16), wp_ref[gi]) * ps_ref[:, c0:c0 + B_GROUP_WIDTH]


def _pool_prompt_kernel(u_ref, halo_ref, wp_ref, ps_ref, o_ref, ext_ref):
    i = pl.program_id(1)
    tb = u_ref.shape[0]
    ext_ref[0:POOL_MAX, :] = jnp.where(i > 0, halo_ref[...], 0.0)
    ext_ref[POOL_MAX:, :] = u_ref[...]
    pos = (i * tb + lax.broadcasted_iota(I32, (tb, 1), 0)).astype(F32)
    for gi, win in enumerate(POOL_WINDOWS):
        c0 = gi * B_GROUP_WIDTH
        c1 = c0 + B_GROUP_WIDTH
        cur = ext_ref[POOL_MAX:, c0:c1]
        ws = cur
        for k in range(1, win):
            ws = ws + ext_ref[POOL_MAX - k:POOL_MAX - k + tb, c0:c1]
        d = ws * (1.0 / jnp.minimum(pos + 1.0, float(win))) - cur
        o_ref[:, c0:c1] = _pool_project(d, gi, wp_ref, ps_ref)


def _pool_prompt(u, wp, ps, nb, tp):
    nt = nb * tp
    tb = QB
    nq = tp // tb
    hb = tb // POOL_MAX
    return pl.pallas_call(
        _pool_prompt_kernel,
        grid=(nb, nq),
        in_specs=[pl.BlockSpec((tb, B_WIDTH), lambda b, i: (b * nq + i, 0)),
                  pl.BlockSpec((POOL_MAX, B_WIDTH), lambda b, i: (jnp.maximum((b * nq + i) * hb - 1, 0), 0)),
                  pl.BlockSpec(wp.shape, lambda b, i: (0, 0, 0)), pl.BlockSpec((1, B_WIDTH), lambda b, i: (0, 0))],
        out_specs=pl.BlockSpec((tb, B_WIDTH), lambda b, i: (b * nq + i, 0)),
        out_shape=jax.ShapeDtypeStruct((nt, B_WIDTH), F32),
        scratch_shapes=[pltpu.VMEM((POOL_MAX + tb, B_WIDTH), F32)],
        compiler_params=_cparams(("parallel", "parallel"), 32),
        name="pool_prompt",
    )(u, u, wp, ps)


def _pool_sample_kernel(ext_ref, wp_ref, ps_ref, o_ref):
    bb, n_ext, _ = ext_ref.shape
    ts = n_ext - (POOL_MAX - 1)
    for gi, win in enumerate(POOL_WINDOWS):
        c0 = gi * B_GROUP_WIDTH
        c1 = c0 + B_GROUP_WIDTH
        cur = ext_ref[:, POOL_MAX - 1:, c0:c1]
        ws = cur
        for k in range(1, win):
            ws = ws + ext_ref[:, POOL_MAX - 1 - k:POOL_MAX - 1 - k + ts, c0:c1]
        d = (ws * (1.0 / win) - cur).reshape(bb * ts, B_GROUP_WIDTH)
        o_ref[:, c0:c1] = _pool_project(d, gi, wp_ref, ps_ref)


def _pool_sample(ext, wp, ps, bb):
    dec_b, n_ext, _ = ext.shape
    ts = n_ext - (POOL_MAX - 1)
    return pl.pallas_call(
        _pool_sample_kernel,
        grid=(dec_b // bb,),
        in_specs=[pl.BlockSpec((bb, n_ext, B_WIDTH), lambda i: (i, 0, 0)),
                  pl.BlockSpec(wp.shape, lambda i: (0, 0, 0)), pl.BlockSpec((1, B_WIDTH), lambda i: (0, 0))],
        out_specs=pl.BlockSpec((bb * ts, B_WIDTH), lambda i: (i, 0)),
        out_shape=jax.ShapeDtypeStruct((dec_b * ts, B_WIDTH), F32),
        compiler_params=_cparams(("parallel",), 32),
        name="pool_sample",
    )(ext, wp, ps)


def _mla_prompt_kernel(q_ref, kc_ref, wuv_ref, o_ref, qs_ref, m_ref, l_ref, acc_ref):
    i = pl.program_id(1)
    n_rows = C_HEADS * QB
    t_i = lax.broadcasted_iota(I32, (n_rows, QB), 0) % QB
    s_i = lax.broadcasted_iota(I32, (n_rows, QB), 1)
    for hd in range(C_HEADS):
        qs_ref[hd * QB:(hd + 1) * QB, :] = q_ref[:, hd * C_QK:(hd + 1) * C_QK].astype(BF16)
    m_ref[...] = jnp.full_like(m_ref, NEG_INF)
    l_ref[...] = jnp.zeros_like(l_ref)
    acc_ref[...] = jnp.zeros_like(acc_ref)

    def step(j, n_blk, mask):
        start = pl.multiple_of(j * QB, QB)
        kc = kc_ref[pl.ds(start, n_blk * QB), :]
        s = _dot_nt(qs_ref[...], kc)
        if mask is not None:
            s = jnp.where(mask, s, NEG_INF)
        s_max = s[:, 0:QB]
        for g in range(1, n_blk):
            s_max = jnp.maximum(s_max, s[:, g * QB:(g + 1) * QB])
        m_old = m_ref[...]
        m_new = jnp.maximum(m_old, jnp.max(s_max, axis=1, keepdims=True))
        alpha = jnp.exp(m_old - m_new)
        p = jnp.exp(s - m_new).astype(BF16)
        l_ref[...] = alpha * l_ref[...] + _dot(p, jnp.ones((n_blk * QB, LANES), BF16))
        acc_ref[...] = alpha * acc_ref[...] + _dot(p, kc[:, 0:C_KV_LORA])
        m_ref[...] = m_new

    def body(jp, c):
        step(2 * jp, 2, None)
        return c

    lax.fori_loop(0, i // 2, body, 0)

    @pl.when(i % 2 == 1)
    def _():
        step(i - 1, 1, None)

    step(i, 1, s_i <= t_i)
    o_lat = (acc_ref[...] * (1.0 / l_ref[:, 0:1])).astype(BF16)
    for hd in range(C_HEADS):
        o_ref[:, hd * C_V_DIM:(hd + 1) * C_V_DIM] = _dot(o_lat[hd * QB:(hd + 1) * QB], wuv_ref[hd])


def _mla_prompt(qc, kc, wuv, nb, tp):
    nt = nb * tp
    nq = tp // QB
    return pl.pallas_call(
        _mla_prompt_kernel,
        grid=(nb, nq),
        in_specs=[pl.BlockSpec((QB, C_HEADS * C_QK), lambda b, i: (b * nq + i, 0)),
                  pl.BlockSpec((tp, C_QK), lambda b, i: (b, 0)),
                  pl.BlockSpec(wuv.shape, lambda b, i: (0, 0, 0))],
        out_specs=pl.BlockSpec((QB, C_WIDTH), lambda b, i: (b * nq + i, 0)),
        out_shape=jax.ShapeDtypeStruct((nt, C_WIDTH), F32),
        scratch_shapes=[pltpu.VMEM((C_HEADS * QB, C_QK), BF16), pltpu.VMEM((C_HEADS * QB, 1), F32),
                        pltpu.VMEM((C_HEADS * QB, LANES), F32), pltpu.VMEM((C_HEADS * QB, C_KV_LORA), F32)],
        compiler_params=_cparams(("parallel", "arbitrary"), 32),
        name="mla_prompt",
    )(qc, kc, wuv)


def _mla_sample_kernel(pt_ref, q_ref, cn_ref, pn_ref, wuv_ref, cc_ref, cp_ref, o_ref, cbuf, pbuf, sem,
                       m_ref, l_ref, acc_ref, *, n_grp, n_pages):
    slot, is_last, drain = _paged_prefetch(pt_ref, (cc_ref, cp_ref), (cbuf, pbuf), sem, n_pages, n_grp, False)
    s = pl.program_id(1)
    ts = q_ref.shape[0]
    rows = C_HEADS * ts
    q = q_ref[...]
    qh = jnp.concatenate([q[:, h * C_QK:(h + 1) * C_QK] for h in range(C_HEADS)], axis=0)
    q_lat = qh[:, 0:C_KV_LORA].astype(BF16)
    q_pe = qh[:, C_KV_LORA:C_KV_LORA + C_ROPE_DIM].astype(BF16)

    def update(sc, v):
        m_old = m_ref[...]
        m_new = jnp.maximum(m_old, jnp.max(sc, axis=1, keepdims=True))
        alpha = jnp.exp(m_old - m_new)
        p = jnp.exp(sc - m_new)
        l_ref[...] = alpha * l_ref[...] + jnp.sum(p, axis=1, keepdims=True)
        acc_ref[...] = alpha * acc_ref[...] + _dot(p.astype(BF16), v)
        m_ref[...] = m_new

    @pl.when(s == 0)
    def _():
        m_ref[...] = jnp.full_like(m_ref, NEG_INF)
        l_ref[...] = jnp.zeros_like(l_ref)
        acc_ref[...] = jnp.zeros_like(acc_ref)
        cn = jnp.concatenate([cn_ref[...], jnp.zeros((PAGE - ts, C_KV_LORA), F32)], axis=0).astype(BF16)
        pn = jnp.concatenate([pn_ref[...], jnp.zeros((PAGE - ts, C_ROPE_DIM), F32)], axis=0).astype(BF16)
        t_i = lax.broadcasted_iota(I32, (rows, PAGE), 0) % ts
        s_i = lax.broadcasted_iota(I32, (rows, PAGE), 1)
        update(jnp.where(s_i <= t_i, _dot_nt(q_lat, cn) + _dot_nt(q_pe, pn), NEG_INF), cn)

    ckv = jnp.concatenate([cbuf[slot, g] for g in range(n_grp)], axis=0).astype(BF16)
    kpe_t = jnp.concatenate([pbuf[slot, g] for g in range(n_grp)], axis=1).astype(BF16)
    update(_dot_nt(q_lat, ckv) + _dot(q_pe, kpe_t), ckv)

    @pl.when(s == pl.num_programs(1) - 1)
    def _():
        o_lat = (acc_ref[...] / l_ref[...]).astype(BF16)
        for hd in range(C_HEADS):
            o_ref[:, hd * C_V_DIM:(hd + 1) * C_V_DIM] = _dot(o_lat[hd * ts:(hd + 1) * ts], wuv_ref[hd])

    @pl.when(is_last)
    def _():
        drain()


def _mla_sample(pt_flat, qc, ckv, kpe, cache_ckv, cache_kpe, wuv, n_prompt_rows, dec_b, ts, n_pages, n_grp):
    row0 = n_prompt_rows // ts
    newrow = lambda b, s, pt: (row0 + b, 0)

    grid_spec = pltpu.PrefetchScalarGridSpec(
        num_scalar_prefetch=1,
        grid=(dec_b, n_pages // n_grp),
        in_specs=[pl.BlockSpec((ts, C_HEADS * C_QK), newrow), pl.BlockSpec((ts, C_KV_LORA), newrow),
                  pl.BlockSpec((ts, C_ROPE_DIM), newrow), pl.BlockSpec(wuv.shape, lambda b, s, pt: (0, 0, 0)),
                  pl.BlockSpec(memory_space=pl.ANY), pl.BlockSpec(memory_space=pl.ANY)],
        out_specs=pl.BlockSpec((ts, C_WIDTH), lambda b, s, pt: (b, 0)),
        scratch_shapes=[pltpu.VMEM((2, n_grp, PAGE, C_KV_LORA), F32), pltpu.VMEM((2, n_grp, C_ROPE_DIM, PAGE), F32),
                        pltpu.SemaphoreType.DMA((2, 2)),
                        pltpu.VMEM((C_HEADS * ts, 1), F32), pltpu.VMEM((C_HEADS * ts, 1), F32),
                        pltpu.VMEM((C_HEADS * ts, C_KV_LORA), F32)],
    )
    return pl.pallas_call(
        functools.partial(_mla_sample_kernel, n_grp=n_grp, n_pages=n_pages),
        grid_spec=grid_spec,
        out_shape=jax.ShapeDtypeStruct((dec_b * ts, C_WIDTH), F32),
        compiler_params=_cparams(("arbitrary", "arbitrary"), 40),
        name="mla_sample",
    )(pt_flat, qc, ckv, kpe, wuv, cache_ckv, cache_kpe)


def _ln_silu(y, g, b):
    yc = y - jnp.mean(y, axis=-1, keepdims=True)
    yn = yc * lax.rsqrt(jnp.mean(yc * yc, axis=-1, keepdims=True) + LN_EPS) * g + b
    return yn * jax.nn.sigmoid(yn)


def _conv_prompt_kernel(u_ref, halo_ref, cw_ref, cb_ref, g_ref, b_ref, o_ref, ext_ref, y_ref):
    i = pl.program_id(1)
    tb = u_ref.shape[0]
    ext_ref[0:CONV_HALO, :] = jnp.where(i > 0, halo_ref[...], 0.0)
    ext_ref[CONV_HALO:, :] = u_ref[...]
    off = CONV_HALO - (CONV_W - 1)
    cw = 256
    for c0 in range(0, D_CONV, cw):
        acc = jnp.zeros((tb, cw), F32) + cb_ref[:, c0:c0 + cw]
        for k in range(CONV_W):
            acc = acc + ext_ref[off + k:off + k + tb, c0:c0 + cw] * cw_ref[k:k + 1, c0:c0 + cw]
        y_ref[:, c0:c0 + cw] = acc
    o_ref[...] = _ln_silu(y_ref[...], g_ref[...], b_ref[...])


def _conv_prompt(u, cw, cb, g, b, nb, tp):
    nt = nb * tp
    tb = QB
    nq = tp // tb
    hb = tb // CONV_HALO
    const = lambda bb, i: (0, 0)
    return pl.pallas_call(
        _conv_prompt_kernel,
        grid=(nb, nq),
        in_specs=[pl.BlockSpec((tb, D_CONV), lambda bb, i: (bb * nq + i, 0)),
                  pl.BlockSpec((CONV_HALO, D_CONV), lambda bb, i: (jnp.maximum((bb * nq + i) * hb - 1, 0), 0)),
                  pl.BlockSpec(cw.shape, const), pl.BlockSpec((1, D_CONV), const), pl.BlockSpec((1, D_CONV), const),
                  pl.BlockSpec((1, D_CONV), const)],
        out_specs=pl.BlockSpec((tb, D_CONV), lambda bb, i: (bb * nq + i, 0)),
        out_shape=jax.ShapeDtypeStruct((nt, D_CONV), F32),
        scratch_shapes=[pltpu.VMEM((CONV_HALO + tb, D_CONV), F32), pltpu.VMEM((tb, D_CONV), F32)],
        compiler_params=_cparams(("parallel", "parallel"), 32),
        name="conv_prompt",
    )(u, u, cw, cb, g, b)


def _conv_sample_kernel(ext_ref, cw_ref, cb_ref, g_ref, b_ref, o_ref):
    bb, n_ext, _ = ext_ref.shape
    ts = n_ext - (CONV_W - 1)
    acc = jnp.zeros((bb, ts, D_CONV), F32) + cb_ref[...]
    for k in range(CONV_W):
        acc = acc + ext_ref[:, k:k + ts, :] * cw_ref[k:k + 1, :]
    o_ref[...] = _ln_silu(acc.reshape(bb * ts, D_CONV), g_ref[...], b_ref[...])


def _conv_sample(ext, cw, cb, g, b, bb):
    dec_b, n_ext, _ = ext.shape
    ts = n_ext - (CONV_W - 1)
    const = lambda i: (0, 0)
    return pl.pallas_call(
        _conv_sample_kernel,
        grid=(dec_b // bb,),
        in_specs=[pl.BlockSpec((bb, n_ext, D_CONV), lambda i: (i, 0, 0)),
                  pl.BlockSpec(cw.shape, const), pl.BlockSpec((1, D_CONV), const), pl.BlockSpec((1, D_CONV), const),
                  pl.BlockSpec((1, D_CONV), const)],
        out_specs=pl.BlockSpec((bb * ts, D_CONV), lambda i: (i, 0)),
        out_shape=jax.ShapeDtypeStruct((dec_b * ts, D_CONV), F32),
        compiler_params=_cparams(("parallel",), 32),
        name="conv_sample",
    )(ext, cw, cb, g, b)


def _router_weights(router_group, router_group_bias, router_expert, router_expert_bias):
    pad = LANES - N_EXPERTS - N_GROUPS
    w = jnp.concatenate([router_expert, router_group, jnp.zeros((D_MODEL, pad), F32)], axis=1)
    b = jnp.concatenate([router_expert_bias, router_group_bias, jnp.zeros((pad,), F32)]).reshape(1, LANES)
    wh = w.astype(BF16)
    wl = (w - wh.astype(F32)).astype(BF16)
    return wh, wl, b


def _cumsum_matrix():
    j = jnp.arange(2 * QB)[:, None] % QB
    c = jnp.arange(2 * QB)[None, :]
    return jnp.where((c >= QB) | (j > c), 1.0, 0.0).astype(BF16)


def kernel(x_prompt, x_sample, cache_sb_k, cache_sb_v, state_pool, cache_mla_ckv, cache_mla_kpe, state_conv, page_table, meta_tokens, norm_mix, norm_ffn, norm_final, w_in_ab, w_pool, pool_scale, w_out_ab, w_in_cd, mla_q_norm, mla_kv_norm, w_uq, w_uk, w_uv, conv_w, conv_b, conv_norm_g, conv_norm_b, w_out_cd, router_group, router_group_bias, router_expert, router_expert_bias, expert_w_gate, expert_w_up, expert_w_down):
    nb, seq, _ = x_prompt.shape
    dec_b, ts, _ = x_sample.shape
    n_pages = page_table.shape[1]
    past_len = n_pages * PAGE
    t_real = seq + N_META
    tp = -(-t_real // QB) * QB
    n_p = nb * tp
    n_s = dec_b * ts
    nt = n_p + n_s
    tm = 256 if nt % 256 == 0 else 128
    assert ts == 8 and n_p % tm == 0 and n_s % tm == 0 and past_len >= POOL_MAX

    meta = jnp.broadcast_to(meta_tokens[None], (nb, N_META, D_MODEL))
    xp = jnp.concatenate([meta, x_prompt, jnp.zeros((nb, tp - t_real, D_MODEL), F32)], axis=1)
    x = jnp.concatenate([xp.reshape(n_p, D_MODEL), x_sample.reshape(n_s, D_MODEL)], axis=0)
    pt_flat = page_table.reshape(-1).astype(I32)
    uo = _cumsum_matrix()
    grp = next(g for g in (SB_PAGES_PER_STEP, 8, 1) if n_pages % g == 0)
    bb = 16 if dec_b % 16 == 0 else dec_b

    def prompt_rows(a, n_last):
        return a[:n_p].reshape(nb, tp, -1)[:, t_real - n_last:t_real]

    def moe_layer(layer, x_res, a_p, a_s, b_p, b_s, w_out):
        rwh, rwl, rb = _router_weights(router_group[layer], router_group_bias[layer], router_expert[layer],
                                       router_expert_bias[layer])
        x_res, hn, ri, rf, cnt = _out_proj(x_res, a_p, a_s, b_p, b_s, w_out.astype(BF16),
                                           norm_ffn[layer].reshape(1, -1), rwh, rwl, rb, tm)
        sched = _moe_schedule(ri, cnt, nt, tm)
        o = _moe(hn, *sched, expert_w_gate, expert_w_up, expert_w_down, layer)
        return x_res, o, rf

    q, k, v, u = _in_proj_ab(x, norm_mix[0].reshape(1, -1), w_in_ab.astype(BF16), tm)
    oa_p = _sb_prompt(q, k, v, uo, nb, tp)
    page_rows = (-1, PAGE * A_KV_HEADS, A_HEAD_DIM)
    oa_s = _sb_sample(pt_flat, q, k, v, cache_sb_k.reshape(page_rows), cache_sb_v.reshape(page_rows), uo, n_p, dec_b, ts,
                      n_pages, grp)
    wp = w_pool.astype(BF16)
    ps = pool_scale.reshape(1, -1)
    ob_p = _pool_prompt(u, wp, ps, nb, tp)
    pool_ext = jnp.concatenate([state_pool, u[n_p:].reshape(dec_b, ts, B_WIDTH)], axis=1)
    ob_s = _pool_sample(pool_ext, wp, ps, bb)
    x, o, rf = moe_layer(0, x, oa_p, oa_s, ob_p, ob_s, w_out_ab)

    o1 = C_Q_LORA
    o2 = o1 + C_KV_LORA
    o3 = o2 + C_ROPE_DIM
    w_cd = jnp.concatenate([w_in_cd[:, :o2], w_in_cd[:, o3:], w_in_cd[:, o2:o3],
                            jnp.zeros((D_MODEL, LANES - C_ROPE_DIM), F32)], axis=1).astype(BF16)
    wq = w_uq.reshape(C_Q_LORA, C_HEADS, C_NOPE_DIM + C_ROPE_DIM)
    wq_pe = jnp.concatenate([wq[:, :, C_NOPE_DIM:], jnp.zeros((C_Q_LORA, C_HEADS, LANES - C_ROPE_DIM), F32)], axis=2)
    wuq = jnp.concatenate([wq[:, :, :C_NOPE_DIM].reshape(C_Q_LORA, -1), wq_pe.reshape(C_Q_LORA, -1)], axis=1).astype(BF16)
    wuk = jnp.transpose(w_uk, (1, 2, 0)).astype(BF16)
    wuv = jnp.transpose(w_uv, (1, 0, 2)).astype(BF16)
    half = C_ROPE_DIM // 2
    inv_freq = ROPE_THETA ** (-jnp.arange(half, dtype=F32) / half)
    pos = jnp.concatenate([jnp.tile(jnp.arange(tp), nb), jnp.tile(past_len + jnp.arange(ts), dec_b)]).astype(F32)
    ang = pos[:, None] * inv_freq[None, :]
    zpad = jnp.zeros((nt, LANES - C_ROPE_DIM), F32)
    cos = jnp.concatenate([jnp.cos(ang), jnp.cos(ang), zpad], axis=1)
    sin = jnp.concatenate([jnp.sin(ang), jnp.sin(ang), zpad], axis=1)
    x, qc, ckv, kpe, kc, uc = _in_proj_cd(x, o, o, rf, norm_mix[1].reshape(1, -1), w_cd, mla_q_norm.reshape(1, -1),
                                          mla_kv_norm.reshape(1, -1), wuq, wuk, cos, sin, tm)
    oc_p = _mla_prompt(qc, kc, wuv, nb, tp)
    grp_c = next(g for g in (MLA_PAGES_PER_STEP, 16, 8, 1) if n_pages % g == 0)
    oc_s = _mla_sample(pt_flat, qc, ckv, kpe, cache_mla_ckv, jnp.swapaxes(cache_mla_kpe, 1, 2), wuv, n_p, dec_b, ts,
                       n_pages, grp_c)
    conv_args = (conv_w, conv_b.reshape(1, -1), conv_norm_g.reshape(1, -1), conv_norm_b.reshape(1, -1))
    od_p = _conv_prompt(uc, *conv_args, nb, tp)
    conv_ext = jnp.concatenate([state_conv, uc[n_p:].reshape(dec_b, ts, D_CONV)], axis=1)
    od_s = _conv_sample(conv_ext, *conv_args, bb)
    x, o, rf = moe_layer(1, x, oc_p, oc_s, od_p, od_s, w_out_cd)

    y = _final_norm(x, o, rf, norm_final.reshape(1, -1), tm)

    y_prompt = y[:n_p].reshape(nb, tp, D_MODEL)[:, N_META:t_real]
    y_sample = y[n_p:].reshape(dec_b, ts, D_MODEL)
    kv4 = (A_KV_HEADS, A_HEAD_DIM)
    return (y_prompt, y_sample,
            prompt_rows(k, t_real).reshape(nb, t_real, *kv4), prompt_rows(v, t_real).reshape(nb, t_real, *kv4),
            k[n_p:].reshape(dec_b, ts, *kv4), v[n_p:].reshape(dec_b, ts, *kv4),
            prompt_rows(u, POOL_MAX - 1), pool_ext[:, ts:],
            prompt_rows(ckv, t_real), prompt_rows(kpe, t_real),
            ckv[n_p:].reshape(dec_b, ts, C_KV_LORA), kpe[n_p:].reshape(dec_b, ts, C_ROPE_DIM),
            prompt_rows(uc, CONV_W - 1), conv_ext[:, ts:])
```
